```python
import jax, jax.numpy as jnp
from jax import lax
import numpy as np

D_MODEL = 2048
BATCH = 16
SEQ = 256
DEPTH = 4
DEC_BATCH = 2
DEC_SEQ = 1024
PAST_LEN = 512

GRID_W = 64
D_ATT = D_MODEL // 2
N_HEADS = 8
HEAD_DIM = D_ATT // N_HEADS
D_CONV = D_MODEL // 4
D_LRU = D_MODEL // 4
D_MIX = D_ATT + D_CONV + D_LRU
D_IN = 3 * D_ATT + 2 * D_CONV + 2 * D_LRU
IN_SPLITS = (D_ATT, 2 * D_ATT, 3 * D_ATT, 3 * D_ATT + D_CONV, 3 * D_ATT + 2 * D_CONV, 3 * D_ATT + 2 * D_CONV + D_LRU)
WIN_ROWS = 8
WIN_COLS = 16
CONV_WIDTH = 31
LRU_CONV_WIDTH = 4
LRU_BLOCKS = 8
LRU_BW = D_LRU // LRU_BLOCKS
LRU_C = 8.0
D_FF = 5632
FFN_CONV_WIDTH = 3
Q_BLOCK = 128
EPS = 1e-6
ATT_SCALE = HEAD_DIM ** -0.5
NEG_INF = -1e30

kernel_name = 'hybrid_flow_trunk_step'


def _rmsnorm(x, g):
    xf = x.astype(jnp.float32)
    y = xf * lax.rsqrt(jnp.mean(xf * xf, axis=-1, keepdims=True) + EPS)
    return (y * g.astype(jnp.float32)).astype(x.dtype)


def _layernorm(x, g, b):
    xf = x.astype(jnp.float32)
    mu = jnp.mean(xf, axis=-1, keepdims=True)
    var = jnp.mean(jnp.square(xf - mu), axis=-1, keepdims=True)
    y = (xf - mu) * lax.rsqrt(var + EPS) * g.astype(jnp.float32) + b.astype(jnp.float32)
    return y.astype(x.dtype)


def _dwconv(x, w, b, pad_l, pad_r):
    C = x.shape[-1]
    y = lax.conv_general_dilated(x, w[:, None, :].astype(x.dtype), window_strides=(1,),
                                 padding=[(pad_l, pad_r)], dimension_numbers=('NWC', 'WIO', 'NWC'),
                                 feature_group_count=C)
    return y + b.astype(x.dtype)


def _context_attention(q, k, v):
    B, S, H, Dh = q.shape
    qb = q.reshape(B, S // Q_BLOCK, Q_BLOCK, H, Dh).swapaxes(0, 1)

    def one(qi):
        s = jnp.einsum('bqhd,bkhd->bhqk', qi, k).astype(jnp.float32) * ATT_SCALE
        p = jax.nn.softmax(s, axis=-1).astype(v.dtype)
        return jnp.einsum('bhqk,bkhd->bqhd', p, v)

    o = lax.map(one, qb)
    return o.swapaxes(0, 1).reshape(B, S, H, Dh)


def _latent_attention(q, k, v, k_ctx, v_ctx, rel_bias):
    B, N, H, Dh = q.shape
    R = N // GRID_W
    wr = min(WIN_ROWS, R)
    rows = jnp.arange(R)
    row_idx = jnp.clip(rows - wr // 2, 0, R - wr)[:, None] + jnp.arange(wr)[None, :]
    cols = jnp.arange(GRID_W)
    col_start = jnp.clip(cols - WIN_COLS // 2, 0, GRID_W - WIN_COLS)
    in_win = (cols[None, :] >= col_start[:, None]) & (cols[None, :] < col_start[:, None] + WIN_COLS)
    ro = row_idx - rows[:, None] + WIN_ROWS - 1
    co = jnp.clip(cols[None, :] - cols[:, None], -(WIN_COLS - 1), WIN_COLS - 1) + WIN_COLS - 1
    bias = rel_bias[:, ro[:, :, None, None], co[None, None, :, :]]
    bias = bias.transpose(1, 0, 3, 2, 4).reshape(R, H, GRID_W, wr * GRID_W).astype(jnp.float32)
    mask = jnp.broadcast_to(in_win[:, None, :], (GRID_W, wr, GRID_W)).reshape(GRID_W, wr * GRID_W)
    n_win = wr * GRID_W
    qg = q.reshape(B, R, GRID_W, H, Dh)
    kg = k.reshape(B, R, GRID_W, H, Dh)[:, row_idx].reshape(B, R, n_win, H, Dh)
    vg = v.reshape(B, R, GRID_W, H, Dh)[:, row_idx].reshape(B, R, n_win, H, Dh)
    s_win = jnp.einsum('brqhd,brkhd->brhqk', qg, kg).astype(jnp.float32) * ATT_SCALE + bias[None]
    s_win = jnp.where(mask, s_win, NEG_INF)
    s_ctx = jnp.einsum('brqhd,bphd->brhqp', qg, k_ctx).astype(jnp.float32) * ATT_SCALE
    p = jax.nn.softmax(jnp.concatenate([s_win, s_ctx], axis=-1), axis=-1).astype(v.dtype)
    o = (jnp.einsum('brhqk,brkhd->brqhd', p[..., :n_win], vg)
         + jnp.einsum('brhqp,bphd->brqhd', p[..., n_win:], v_ctx))
    return o.reshape(B, N, H, Dh)


def _conv_module(a, g, p):
    u = a * jax.nn.sigmoid(g)
    u = _dwconv(u, p['cv_w'], p['cv_b'], CONV_WIDTH // 2, CONV_WIDTH // 2)
    u = _layernorm(u, p['cv_ln_g'], p['cv_ln_b'])
    return jax.nn.silu(u)


def _blockdiag(x, w, b):
    B, S, C = x.shape
    y = jnp.einsum('bsnc,ncd->bsnd', x.reshape(B, S, LRU_BLOCKS, LRU_BW), w).reshape(B, S, C)
    return (y + b).astype(jnp.float32)


def _rglru_dir(x, w_a, b_a, w_i, b_i, lam, h0, reverse):
    r = jax.nn.sigmoid(_blockdiag(x, w_a, b_a))
    i = jax.nn.sigmoid(_blockdiag(x, w_i, b_i))
    log_a = -LRU_C * r * jax.nn.softplus(-lam.astype(jnp.float32))
    a = jnp.exp(log_a)
    bx = jnp.sqrt(-jnp.expm1(2.0 * log_a)) * (i * x.astype(jnp.float32))

    def step(h, ab):
        h = ab[0] * h + ab[1]
        return h, h

    h_last, hs = lax.scan(step, h0.astype(jnp.float32), (a.swapaxes(0, 1), bx.swapaxes(0, 1)), reverse=reverse)
    return hs.swapaxes(0, 1), h_last


def _recurrent(xr, gate, p, h0):
    xc = _dwconv(xr, p['lru_conv_w'], p['lru_conv_b'], LRU_CONV_WIDTH // 2, LRU_CONV_WIDTH - 1 - LRU_CONV_WIDTH // 2)
    hf, lf = _rglru_dir(xc, p['lru_wa'][0], p['lru_ba'][0], p['lru_wi'][0], p['lru_bi'][0], p['lru_lam'][0], h0[:, 0], False)
    hb, lb = _rglru_dir(xc, p['lru_wa'][1], p['lru_ba'][1], p['lru_wi'][1], p['lru_bi'][1], p['lru_lam'][1], h0[:, 1], True)
    y = (hf + hb).astype(xr.dtype) * jax.nn.gelu(gate)
    return y, jnp.stack([lf, lb], axis=1)


def _layer(x, cond, p, ctx):
    B, S, _ = x.shape
    mod = (jax.nn.silu(cond) @ p['w_mod'] + p['b_mod'])[:, None, :]
    sh1, sc1, g1, sh2, sc2, g2 = jnp.split(mod, 6, axis=-1)
    h = _rmsnorm(x, p['ln1_g']) * (1 + sc1) + sh1
    z = h @ p['w_in']
    q, k, v, cva, cvg, lx, lg = jnp.split(z, IN_SPLITS, axis=-1)
    q = q.reshape(B, S, N_HEADS, HEAD_DIM)
    k = k.reshape(B, S, N_HEADS, HEAD_DIM)
    v = v.reshape(B, S, N_HEADS, HEAD_DIM)
    if ctx is None:
        att = _context_attention(q, k, v)
        h0 = jnp.zeros((B, 2, D_LRU), jnp.float32)
    else:
        k_ctx, v_ctx, h0 = ctx
        att = _latent_attention(q, k, v, k_ctx, v_ctx, p['na_bias'])
    cv = _conv_module(cva, cvg, p)
    rec, h_last = _recurrent(lx, lg, p, h0)
    mix = jnp.concatenate([att.reshape(B, S, D_ATT), cv, rec], axis=-1) @ p['w_out']
    x = x + g1 * mix
    h = _rmsnorm(x, p['ln2_g']) * (1 + sc2) + sh2
    u = _dwconv(h @ p['ffn_up'], p['ffn_conv_w'], p['ffn_conv_b'], FFN_CONV_WIDTH // 2, FFN_CONV_WIDTH // 2)
    ug, uv = jnp.split(u, 2, axis=-1)
    x = x + g2 * ((jax.nn.silu(ug) * uv) @ p['ffn_down'])
    return x, k, v, h_last


def setup_inputs(seed: int = 0) -> dict:
    key = jax.random.key(seed)
    ks = jax.random.split(key, 48)
    cnt = [0]

    def nxt():
        kk = ks[cnt[0]]
        cnt[0] += 1
        return kk

    def nrm(shape, scale=1.0):
        return jax.random.normal(nxt(), shape, jnp.float32) * scale

    L, D = DEPTH, D_MODEL
    u = jax.random.uniform(nxt(), (L, 2, D_LRU), jnp.float32, 0.9, 0.999)
    a_base = u ** (1.0 / LRU_C)
    lru_lam = jnp.log(a_base) - jnp.log1p(-a_base)
    return {
        'x_prompt': nrm((BATCH, SEQ, D)),
        'x_sample': nrm((DEC_BATCH, DEC_SEQ, D)),
        'cache_k': nrm((DEC_BATCH, DEPTH, PAST_LEN, N_HEADS, HEAD_DIM)),
        'cache_v': nrm((DEC_BATCH, DEPTH, PAST_LEN, N_HEADS, HEAD_DIM)),
        'state_lru': nrm((DEC_BATCH, DEPTH, 2, D_LRU), 0.5),
        'c': nrm((DEC_BATCH, D)),
        'c_ctx': nrm((D,)),
        'ln1_g': 1.0 + nrm((L, D), 0.02),
        'w_mod': nrm((L, D, 6 * D), 0.5 * D ** -0.5),
        'b_mod': nrm((L, 6 * D), 0.02),
        'w_in': nrm((L, D, D_IN), D ** -0.5),
        'na_bias': nrm((L, N_HEADS, 2 * WIN_ROWS - 1, 2 * WIN_COLS - 1), 0.5),
        'cv_w': nrm((L, CONV_WIDTH, D_CONV), CONV_WIDTH ** -0.5),
        'cv_b': nrm((L, D_CONV), 0.02),
        'cv_ln_g': 1.0 + nrm((L, D_CONV), 0.02),
        'cv_ln_b': nrm((L, D_CONV), 0.02),
        'lru_conv_w': nrm((L, LRU_CONV_WIDTH, D_LRU), LRU_CONV_WIDTH ** -0.5),
        'lru_conv_b': nrm((L, D_LRU), 0.02),
        'lru_wa': nrm((L, 2, LRU_BLOCKS, LRU_BW, LRU_BW), LRU_BW ** -0.5),
        'lru_ba': nrm((L, 2, D_LRU), 0.02),
        'lru_wi': nrm((L, 2, LRU_BLOCKS, LRU_BW, LRU_BW), LRU_BW ** -0.5),
        'lru_bi': nrm((L, 2, D_LRU), 0.02),
        'lru_lam': lru_lam,
        'w_out': nrm((L, D_MIX, D), D_MIX ** -0.5),
        'ln2_g': 1.0 + nrm((L, D), 0.02),
        'ffn_up': nrm((L, D, 2 * D_FF), D ** -0.5),
        'ffn_conv_w': nrm((L, FFN_CONV_WIDTH, 2 * D_FF), FFN_CONV_WIDTH ** -0.5),
        'ffn_conv_b': nrm((L, 2 * D_FF), 0.02),
        'ffn_down': nrm((L, D_FF, D), D_FF ** -0.5),
        'final_g': 1.0 + nrm((D,), 0.02),
    }


def reference(x_prompt, x_sample, cache_k, cache_v, state_lru, c, c_ctx, ln1_g, w_mod, b_mod, w_in, na_bias,
              cv_w, cv_b, cv_ln_g, cv_ln_b, lru_conv_w, lru_conv_b, lru_wa, lru_ba, lru_wi, lru_bi, lru_lam,
              w_out, ln2_g, ffn_up, ffn_conv_w, ffn_conv_b, ffn_down, final_g):
    yp = x_prompt
    ys = x_sample
    cond_ctx = c_ctx[None, :]
    ks, vs, hs = [], [], []
    for l in range(DEPTH):
        p = {'ln1_g': ln1_g[l], 'w_mod': w_mod[l], 'b_mod': b_mod[l], 'w_in': w_in[l], 'na_bias': na_bias[l],
             'cv_w': cv_w[l], 'cv_b': cv_b[l], 'cv_ln_g': cv_ln_g[l], 'cv_ln_b': cv_ln_b[l],
             'lru_conv_w': lru_conv_w[l], 'lru_conv_b': lru_conv_b[l], 'lru_wa': lru_wa[l], 'lru_ba': lru_ba[l],
             'lru_wi': lru_wi[l], 'lru_bi': lru_bi[l], 'lru_lam': lru_lam[l], 'w_out': w_out[l],
             'ln2_g': ln2_g[l], 'ffn_up': ffn_up[l], 'ffn_conv_w': ffn_conv_w[l], 'ffn_conv_b': ffn_conv_b[l],
             'ffn_down': ffn_down[l]}
        yp, k_l, v_l, h_l = _layer(yp, cond_ctx, p, None)
        ks.append(k_l)
        vs.append(v_l)
        hs.append(h_l)
        ys, _, _, _ = _layer(ys, c, p, (cache_k[:, l], cache_v[:, l], state_lru[:, l]))
    y_prompt = _rmsnorm(yp, final_g)
    y_sample = _rmsnorm(ys, final_g)
    new_k = jnp.stack(ks, axis=1)
    new_v = jnp.stack(vs, axis=1)
    new_lru = jnp.stack(hs, axis=1)
    return (y_prompt, y_sample, new_k, new_v, new_lru)
```

```python
import functools

import jax
import jax.numpy as jnp
from jax import lax
from jax.experimental import pallas as pl
from jax.experimental.pallas import tpu as pltpu

F32 = jnp.float32
BF16 = jnp.bfloat16

D_MODEL = 2048
BATCH = 16
SEQ = 256
DEPTH = 4
DEC_BATCH = 2
DEC_SEQ = 1024
PAST_LEN = 512
GRID_W = 64
GRID_R = DEC_SEQ // GRID_W
D_ATT = 1024
N_HEADS = 8
HEAD_DIM = 128
D_CONV = 512
D_LRU = 512
D_MIX = D_ATT + D_CONV + D_LRU
D_IN = 3 * D_ATT + 2 * D_CONV + 2 * D_LRU
WIN_ROWS = 8
WIN_COLS = 16
CONV_WIDTH = 31
LRU_CONV_WIDTH = 4
LRU_BLOCKS = 8
LRU_BW = D_LRU // LRU_BLOCKS
LRU_C = 8.0
D_FF = 5632
EPS = 1e-6
ATT_SCALE = HEAD_DIM ** -0.5
NEG_INF = -1e30

T_CTX = BATCH * SEQ
T_LAT = DEC_BATCH * DEC_SEQ
T_ALL = T_CTX + T_LAT

V7X_VMEM_BYTES = 64 * 1024 * 1024
SUBLANES = 8
LANES = 128

TM = 1024
N_MT = T_ALL // TM
CTX_MT = T_CTX // TM
TN_IN = 512
TN_OUT = 512
TF = 256
N_FT = D_FF // TF
TN_MOD = 1024
NORM_ROWS = 128
N_COND = 8
BIAS_ROWS = 2 * WIN_ROWS - 1
BIAS_COLS = 2 * WIN_COLS - 1


def _cparams(n_axes, vmem_mib):
    return pltpu.CompilerParams(
        dimension_semantics=("arbitrary",) * n_axes,
        vmem_limit_bytes=min(vmem_mib * 1024 * 1024, V7X_VMEM_BYTES - 2 * 1024 * 1024),
    )


def _sigmoid(x):
    return 0.5 * jnp.tanh(0.5 * x) + 0.5


def _silu(x):
    return x * _sigmoid(x)


def _gelu_tanh(x):
    return 0.5 * x * (1.0 + jnp.tanh(0.7978845608028654 * (x + 0.044715 * (x * x * x))))


def _mod_row(i):
    return jnp.maximum(i - (CTX_MT - 1), 0)


def _mod_kernel(c_ref, w_ref, b_ref, o_ref):
    a = _silu(c_ref[...]).astype(BF16)
    o_ref[...] = jnp.dot(a, w_ref[...].astype(BF16), preferred_element_type=F32) + b_ref[...]


def _modulation(cond, w_mod, b_mod):
    n_out = w_mod.shape[-1]
    return pl.pallas_call(
        _mod_kernel,
        out_shape=jax.ShapeDtypeStruct((DEPTH, N_COND, n_out), F32),
        grid=(DEPTH, n_out // TN_MOD),
        in_specs=[
            pl.BlockSpec((N_COND, D_MODEL), lambda l, j: (0, 0)),
            pl.BlockSpec((None, D_MODEL, TN_MOD), lambda l, j: (l, 0, j)),
            pl.BlockSpec((None, 1, TN_MOD), lambda l, j: (l, 0, j)),
        ],
        out_specs=pl.BlockSpec((None, N_COND, TN_MOD), lambda l, j: (l, 0, j)),
        compiler_params=_cparams(2, 40),
        name="modulation",
    )(cond, w_mod, b_mod.reshape(DEPTH, 1, n_out))


def _bias_kernel(rb_ref, o_ref, td_scr):
    base = (pl.program_id(0) * N_HEADS + pl.program_id(1)) * (BIAS_ROWS * BIAS_COLS)
    qi = lax.broadcasted_iota(jnp.int32, (GRID_W, LANES), 0)
    li = lax.broadcasted_iota(jnp.int32, (GRID_W, LANES), 1)
    diff = jnp.clip((li & (GRID_W - 1)) - qi, -(WIN_COLS - 1), WIN_COLS - 1) + (WIN_COLS - 1)

    def body(i, carry):
        acc = jnp.zeros((GRID_W, LANES), F32)
        for d in range(BIAS_COLS):
            acc = jnp.where(diff == d, rb_ref[base + i * BIAS_COLS + d], acc)
        td_scr[i] = acc
        return carry

    lax.fori_loop(0, BIAS_ROWS, body, 0)
    td_scr[BIAS_ROWS] = jnp.zeros((GRID_W, LANES), F32)
    left = li < GRID_W
    for i in range(BIAS_ROWS):
        o_ref[i] = jnp.where(left, td_scr[i], td_scr[i + 1])


def _bias_table(na_bias):
    return pl.pallas_call(
        _bias_kernel,
        out_shape=jax.ShapeDtypeStruct((DEPTH, N_HEADS, BIAS_ROWS, GRID_W, LANES), F32),
        grid=(DEPTH, N_HEADS),
        in_specs=[pl.BlockSpec(memory_space=pltpu.SMEM)],
        out_specs=pl.BlockSpec((None, None, BIAS_ROWS, GRID_W, LANES), lambda l, h: (l, h, 0, 0, 0)),
        scratch_shapes=[pltpu.VMEM((BIAS_ROWS + 1, GRID_W, LANES), F32)],
        compiler_params=_cparams(2, 16),
        name="bias_table",
    )(na_bias.reshape(-1))


def _norm_mod(x_ref, g_ref, sc_ref, sh_ref, h_scr):
    g = g_ref[...]
    scale = 1.0 + sc_ref[...]
    shift = sh_ref[...]

    def body(c, carry):
        r0 = pl.multiple_of(c * NORM_ROWS, NORM_ROWS)
        x = x_ref[pl.ds(r0, NORM_ROWS), :]
        ms = jnp.mean(x * x, axis=-1, keepdims=True)
        y = x * lax.rsqrt(ms + EPS) * g
        h_scr[pl.ds(r0, NORM_ROWS), :] = (y * scale + shift).astype(BF16)
        return carry

    lax.fori_loop(0, TM // NORM_ROWS, body, 0)


def _mod_spec(chunk):
    return pl.BlockSpec((None, None, 1, D_MODEL), lambda i, j, l: (l[0], _mod_row(i), 0, chunk))


def _inproj_kernel(l_ref, x_ref, g_ref, sh_ref, sc_ref, w_ref, o_ref, h_scr):
    @pl.when(pl.program_id(1) == 0)
    def _():
        _norm_mod(x_ref, g_ref, sc_ref, sh_ref, h_scr)

    o_ref[...] = jnp.dot(h_scr[...], w_ref[...].astype(BF16), preferred_element_type=F32)


def _inproj(l, x, ln_g, mod4, w_in):
    return pl.pallas_call(
        _inproj_kernel,
        out_shape=jax.ShapeDtypeStruct((T_ALL, D_IN), F32),
        grid_spec=pltpu.PrefetchScalarGridSpec(
            num_scalar_prefetch=1,
            grid=(N_MT, D_IN // TN_IN),
            in_specs=[
                pl.BlockSpec((TM, D_MODEL), lambda i, j, l: (i, 0)),
                pl.BlockSpec((None, 1, D_MODEL), lambda i, j, l: (l[0], 0, 0)),
                _mod_spec(0),
                _mod_spec(1),
                pl.BlockSpec((None, D_MODEL, TN_IN), lambda i, j, l: (l[0], 0, j)),
            ],
            out_specs=pl.BlockSpec((TM, TN_IN), lambda i, j, l: (i, j)),
            scratch_shapes=[pltpu.VMEM((TM, D_MODEL), BF16)],
        ),
        compiler_params=_cparams(2, 48),
        name="inproj",
    )(l, x, ln_g, mod4, mod4, w_in)


def _dot_nt(a, b):
    return lax.dot_general(a, b, (((1,), (1,)), ((), ())), preferred_element_type=F32)


def _att_ctx_kernel(q_ref, k_ref, v_ref, o_ref):
    for h in range(N_HEADS):
        sl = slice(h * HEAD_DIM, (h + 1) * HEAD_DIM)
        q = q_ref[:, sl].astype(BF16)
        k = k_ref[:, sl].astype(BF16)
        v = v_ref[:, sl].astype(BF16)
        s = _dot_nt(q, k) * ATT_SCALE
        p = jnp.exp(s - jnp.max(s, axis=-1, keepdims=True))
        denom = jnp.sum(p, axis=-1, keepdims=True)
        o = jnp.dot(p.astype(BF16), v, preferred_element_type=F32)
        o_ref[:, sl] = (o / denom).astype(o_ref.dtype)


def _att_ctx(z):
    return pl.pallas_call(
        _att_ctx_kernel,
        out_shape=jax.ShapeDtypeStruct((T_ALL, D_ATT), BF16),
        grid=(BATCH,),
        in_specs=[
            pl.BlockSpec((SEQ, D_ATT), lambda b: (b, 0)),
            pl.BlockSpec((SEQ, D_ATT), lambda b: (b, 1)),
            pl.BlockSpec((SEQ, D_ATT), lambda b: (b, 2)),
        ],
        out_specs=pl.BlockSpec((SEQ, D_ATT), lambda b: (b, 0)),
        compiler_params=_cparams(1, 32),
        name="att_ctx",
    )(z, z, z)


Q_ROWS = 4
_KEY_ROWS = ((0, 8), (0, 12), (4, 16), (8, 16))


def _win_start(rq):
    return min(max(rq - WIN_ROWS // 2, 0), GRID_R - WIN_ROWS)


def _att_lat_kernel(l_ref, q_ref, k_ref, v_ref, ck_ref, cv_ref, tt_ref, att_in_ref, o_ref):
    del l_ref, att_in_ref
    qi = lax.broadcasted_iota(jnp.int32, (GRID_W, LANES), 0)
    li = lax.broadcasted_iota(jnp.int32, (GRID_W, LANES), 1)
    kc = li & (GRID_W - 1)
    cs = jnp.clip(qi - WIN_COLS // 2, 0, GRID_W - WIN_COLS)
    in_cols = (kc >= cs) & (kc < cs + WIN_COLS)
    left = li < GRID_W
    mask_both = in_cols
    mask_left = in_cols & left
    mask_right = in_cols & jnp.logical_not(left)
    neg = jnp.full((GRID_W, LANES), NEG_INF, F32)

    kb = k_ref[...].astype(BF16)
    vb = v_ref[...].astype(BF16)
    ckb = ck_ref[...].astype(BF16)
    cvb = cv_ref[...].astype(BF16)
    tq = Q_ROWS * GRID_W
    for t in range(GRID_R // Q_ROWS):
        lo, hi = _KEY_ROWS[t]
        q = q_ref[t * tq:(t + 1) * tq, :].astype(BF16)
        s_w = _dot_nt(q, kb[lo * GRID_W:hi * GRID_W, :])
        s_c = _dot_nt(q, ckb) * ATT_SCALE
        rows = []
        for rl in range(Q_ROWS):
            rq = t * Q_ROWS + rl
            start = _win_start(rq)
            blks = []
            for m in range(lo // 2, hi // 2):
                v0 = start <= 2 * m < start + WIN_ROWS
                v1 = start <= 2 * m + 1 < start + WIN_ROWS
                if not (v0 or v1):
                    blks.append(neg)
                    continue
                i = 2 * m - rq + WIN_ROWS - 1
                assert 0 <= i < BIAS_ROWS
                mask = mask_both if (v0 and v1) else (mask_left if v0 else mask_right)
                c0 = (2 * m - lo) * GRID_W
                sb = s_w[rl * GRID_W:(rl + 1) * GRID_W, c0:c0 + LANES]
                blks.append(jnp.where(mask, sb * ATT_SCALE + tt_ref[i], NEG_INF))
            rows.append(jnp.concatenate(blks, axis=1))
        sw = jnp.concatenate(rows, axis=0)
        mx = jnp.maximum(jnp.max(sw, axis=-1, keepdims=True), jnp.max(s_c, axis=-1, keepdims=True))
        pw = jnp.exp(sw - mx)
        pc = jnp.exp(s_c - mx)
        denom = jnp.sum(pw, axis=-1, keepdims=True) + jnp.sum(pc, axis=-1, keepdims=True)
        o = jnp.dot(pw.astype(BF16), vb[lo * GRID_W:hi * GRID_W, :], preferred_element_type=F32)
        o = o + jnp.dot(pc.astype(BF16), cvb, preferred_element_type=F32)
        o_ref[t * tq:(t + 1) * tq, :] = (o / denom).astype(o_ref.dtype)


def _att_lat(l, z, cache_k, cache_v, tt, att):
    rb0 = T_CTX // DEC_SEQ
    hb = D_ATT // HEAD_DIM
    return pl.pallas_call(
        _att_lat_kernel,
        out_shape=jax.ShapeDtypeStruct((T_ALL, D_ATT), BF16),
        grid_spec=pltpu.PrefetchScalarGridSpec(
            num_scalar_prefetch=1,
            grid=(DEC_BATCH, N_HEADS),
            in_specs=[
                pl.BlockSpec((DEC_SEQ, HEAD_DIM), lambda b, h, l: (rb0 + b, h)),
                pl.BlockSpec((DEC_SEQ, HEAD_DIM), lambda b, h, l: (rb0 + b, hb + h)),
                pl.BlockSpec((DEC_SEQ, HEAD_DIM), lambda b, h, l: (rb0 + b, 2 * hb + h)),
                pl.BlockSpec((None, None, PAST_LEN, HEAD_DIM), lambda b, h, l: (b, l[0], 0, h)),
                pl.BlockSpec((None, None, PAST_LEN, HEAD_DIM), lambda b, h, l: (b, l[0], 0, h)),
                pl.BlockSpec((None, None, BIAS_ROWS, GRID_W, LANES), lambda b, h, l: (l[0], h, 0, 0, 0)),
                pl.BlockSpec(memory_space=pl.ANY),
            ],
            out_specs=pl.BlockSpec((DEC_SEQ, HEAD_DIM), lambda b, h, l: (rb0 + b, h)),
        ),
        input_output_aliases={7: 0},
        compiler_params=_cparams(2, 32),
        name="att_lat",
    )(l, z, z, z, cache_k, cache_v, tt, att)


CV_PAD = 16
CV_ROWS = 64


def _convmod_kernel(l_ref, a_ref, g_ref, w_ref, b_ref, lng_ref, lnb_ref, *rest):
    del l_ref
    o_ref, upad, ush = rest[-3:]
    s_len = a_ref.shape[0]
    n_sh = s_len + CV_PAD + SUBLANES
    upad[0:CV_PAD, :] = jnp.zeros((CV_PAD, D_CONV), F32)
    upad[CV_PAD + s_len:, :] = jnp.zeros((CV_PAD, D_CONV), F32)
    upad[CV_PAD:CV_PAD + s_len, :] = a_ref[...] * _sigmoid(g_ref[...])
    for j in range(SUBLANES):
        ush[j, 0:n_sh, :] = upad[j:j + n_sh, :]
    bias = b_ref[...]
    ln_g = lng_ref[...]
    ln_b = lnb_ref[...]

    def chunk(c, carry):
        r0 = pl.multiple_of(c * CV_ROWS, CV_ROWS)
        acc = jnp.broadcast_to(bias, (CV_ROWS, D_CONV))
        for k in range(CONV_WIDTH):
            m, j = divmod(k + CV_PAD - CONV_WIDTH // 2, SUBLANES)
            acc = acc + w_ref[k:k + 1, :] * ush[j, pl.ds(r0 + SUBLANES * m, CV_ROWS), :]
        mu = jnp.mean(acc, axis=-1, keepdims=True)
        xc = acc - mu
        var = jnp.mean(xc * xc, axis=-1, keepdims=True)
        y = xc * lax.rsqrt(var + EPS) * ln_g + ln_b
        o_ref[pl.ds(r0, CV_ROWS), :] = _silu(y).astype(o_ref.dtype)
        return carry

    lax.fori_loop(0, s_len // CV_ROWS, chunk, 0)


def _convmod(l, z, cv_w, cv_b, ln_g, ln_b, s_len, n_seq, row_block0, prev=None):
    cb = (3 * D_ATT) // D_CONV
    in_specs = [
        pl.BlockSpec((s_len, D_CONV), lambda b, l: (row_block0 + b, cb)),
        pl.BlockSpec((s_len, D_CONV), lambda b, l: (row_block0 + b, cb + 1)),
        pl.BlockSpec((None, CONV_WIDTH, D_CONV), lambda b, l: (l[0], 0, 0)),
        pl.BlockSpec((None, 1, D_CONV), lambda b, l: (l[0], 0, 0)),
        pl.BlockSpec((None, 1, D_CONV), lambda b, l: (l[0], 0, 0)),
        pl.BlockSpec((None, 1, D_CONV), lambda b, l: (l[0], 0, 0)),
    ]
    args = [l, z, z, cv_w, cv_b, ln_g, ln_b]
    aliases = {}
    if prev is not None:
        in_specs.append(pl.BlockSpec(memory_space=pl.ANY))
        args.append(prev)
        aliases = {len(args) - 1: 0}
    return pl.pallas_call(
        _convmod_kernel,
        out_shape=jax.ShapeDtypeStruct((T_ALL, D_CONV), BF16),
        grid_spec=pltpu.PrefetchScalarGridSpec(
            num_scalar_prefetch=1,
            grid=(n_seq,),
            in_specs=in_specs,
            out_specs=pl.BlockSpec((s_len, D_CONV), lambda b, l: (row_block0 + b, 0)),
            scratch_shapes=[
                pltpu.VMEM((s_len + 2 * CV_PAD, D_CONV), F32),
                pltpu.VMEM((SUBLANES, s_len + CV_PAD + SUBLANES, D_CONV), F32),
            ],
        ),
        input_output_aliases=aliases,
        compiler_params=_cparams(1, 48),
        name=f"convmod_{s_len}",
    )(*args)


LRU_PAD = 8
LRU_HALF = D_LRU // 2
LRU_CHUNK = 256


def _softplus(x):
    return jnp.maximum(x, 0.0) + jnp.log1p(jnp.exp(-jnp.abs(x)))


def _lru_kernel(l_ref, x_ref, g_ref, h0_ref, cw_ref, cb_ref, wa_ref, ba_ref, wi_ref, bi_ref, lam_ref, *rest):
    del l_ref
    y_ref, hl_ref, xpad, a_scr, b_scr, h_scr = rest[-6:]
    s_len = x_ref.shape[0]
    n_blk = s_len // SUBLANES
    n_lt = D_LRU // LANES

    xpad[0:LRU_PAD, :] = jnp.zeros((LRU_PAD, D_LRU), F32)
    xpad[LRU_PAD + s_len:, :] = jnp.zeros((LRU_PAD, D_LRU), F32)
    xpad[LRU_PAD:LRU_PAD + s_len, :] = x_ref[...]
    cb = cb_ref[...]
    decay = [-LRU_C * _softplus(-lam_ref[d:d + 1, :]) for d in range(2)]

    for c in range(s_len // LRU_CHUNK):
        r0 = c * LRU_CHUNK
        xc = jnp.broadcast_to(cb, (LRU_CHUNK, D_LRU))
        for k in range(LRU_CONV_WIDTH):
            off = LRU_PAD + r0 + k - LRU_CONV_WIDTH // 2
            xc = xc + cw_ref[k:k + 1, :] * xpad[off:off + LRU_CHUNK, :]
        xb = xc.astype(BF16)
        for d in range(2):
            ga = jnp.concatenate(
                [jnp.dot(xb[:, hh * LRU_HALF:(hh + 1) * LRU_HALF], wa_ref[d, hh], preferred_element_type=F32)
                 for hh in range(2)], axis=1) + ba_ref[d:d + 1, :]
            gi = jnp.concatenate(
                [jnp.dot(xb[:, hh * LRU_HALF:(hh + 1) * LRU_HALF], wi_ref[d, hh], preferred_element_type=F32)
                 for hh in range(2)], axis=1) + bi_ref[d:d + 1, :]
            log_a = _sigmoid(ga) * decay[d]
            th = jnp.tanh(log_a)
            gain = jnp.sqrt(-2.0 * th / (1.0 - th))
            a_scr[d, r0:r0 + LRU_CHUNK, :] = jnp.exp(log_a)
            b_scr[d, r0:r0 + LRU_CHUNK, :] = gain * (_sigmoid(gi) * xc)

    row = lax.broadcasted_iota(jnp.int32, (SUBLANES, LANES), 0)

    def scan_block(it, carry):
        new = []
        for d in range(2):
            blk = it if d == 0 else n_blk - 1 - it
            r0 = pl.multiple_of(blk * SUBLANES, SUBLANES)
            for c in range(n_lt):
                ls = slice(c * LANES, (c + 1) * LANES)
                a = a_scr[d, pl.ds(r0, SUBLANES), ls]
                b = b_scr[d, pl.ds(r0, SUBLANES), ls]
                for s in (1, 2, 4):
                    shift = s if d == 0 else SUBLANES - s
                    valid = (row >= s) if d == 0 else (row < SUBLANES - s)
                    a_sh = jnp.where(valid, pltpu.roll(a, shift, 0), 1.0)
                    b_sh = jnp.where(valid, pltpu.roll(b, shift, 0), 0.0)
                    b = a * b_sh + b
                    a = a * a_sh
                h = a * carry[d * n_lt + c] + b
                h_scr[d, pl.ds(r0, SUBLANES), ls] = h
                last = h[SUBLANES - 1:SUBLANES, :] if d == 0 else h[0:1, :]
                new.append(jnp.broadcast_to(last, (SUBLANES, LANES)))
        return tuple(new)

    init = tuple(
        jnp.broadcast_to(h0_ref[d:d + 1, c * LANES:(c + 1) * LANES], (SUBLANES, LANES))
        for d in range(2) for c in range(n_lt))
    final = lax.fori_loop(0, n_blk, scan_block, init)
    for d in range(2):
        for c in range(n_lt):
            hl_ref[d:d + 1, c * LANES:(c + 1) * LANES] = final[d * n_lt + c][0:1, :]

    for c in range(s_len // LRU_CHUNK):
        rs = slice(c * LRU_CHUNK, (c + 1) * LRU_CHUNK)
        y = (h_scr[0, rs, :] + h_scr[1, rs, :]) * _gelu_tanh(g_ref[rs, :])
        y_ref[rs, :] = y.astype(y_ref.dtype)


def _lru(l, z, h0, p, s_len, n_seq, row_block0, prev=None):
    cb = (3 * D_ATT + 2 * D_CONV) // D_LRU
    in_specs = [
        pl.BlockSpec((s_len, D_LRU), lambda b, l: (row_block0 + b, cb)),
        pl.BlockSpec((s_len, D_LRU), lambda b, l: (row_block0 + b, cb + 1)),
        pl.BlockSpec((None, 2, D_LRU), lambda b, l: (b, 0, 0)),
        pl.BlockSpec((None, LRU_CONV_WIDTH, D_LRU), lambda b, l: (l[0], 0, 0)),
        pl.BlockSpec((None, 1, D_LRU), lambda b, l: (l[0], 0, 0)),
        pl.BlockSpec((None, 2, 2, LRU_HALF, LRU_HALF), lambda b, l: (l[0], 0, 0, 0, 0)),
        pl.BlockSpec((None, 2, D_LRU), lambda b, l: (l[0], 0, 0)),
        pl.BlockSpec((None, 2, 2, LRU_HALF, LRU_HALF), lambda b, l: (l[0], 0, 0, 0, 0)),
        pl.BlockSpec((None, 2, D_LRU), lambda b, l: (l[0], 0, 0)),
        pl.BlockSpec((None, 2, D_LRU), lambda b, l: (l[0], 0, 0)),
    ]
    args = [l, z, z, h0, p["lru_conv_w"], p["lru_conv_b"], p["wa_dense"], p["lru_ba"], p["wi_dense"],
            p["lru_bi"], p["lru_lam"]]
    aliases = {}
    if prev is not None:
        in_specs.append(pl.BlockSpec(memory_space=pl.ANY))
        args.append(prev)
        aliases = {len(args) - 1: 0}
    return pl.pallas_call(
        _lru_kernel,
        out_shape=(jax.ShapeDtypeStruct((T_ALL, D_LRU), BF16),
                   jax.ShapeDtypeStruct((n_seq, 2, D_LRU), F32)),
        grid_spec=pltpu.PrefetchScalarGridSpec(
            num_scalar_prefetch=1,
            grid=(n_seq,),
            in_specs=in_specs,
            out_specs=(pl.BlockSpec((s_len, D_LRU), lambda b, l: (row_block0 + b, 0)),
                       pl.BlockSpec((None, 2, D_LRU), lambda b, l: (b, 0, 0))),
            scratch_shapes=[
                pltpu.VMEM((s_len + 2 * LRU_PAD, D_LRU), F32),
                pltpu.VMEM((2, s_len, D_LRU), F32),
                pltpu.VMEM((2, s_len, D_LRU), F32),
                pltpu.VMEM((2, s_len, D_LRU), F32),
            ],
        ),
        input_output_aliases=aliases,
        compiler_params=_cparams(1, 48),
        name=f"rglru_{s_len}",
    )(*args)


def _outproj_kernel(l_ref, att_ref, cv_ref, rec_ref, w_ref, x_ref, gate_ref, o_ref):
    del l_ref
    y = jnp.dot(att_ref[...], w_ref[0:D_ATT, :].astype(BF16), preferred_element_type=F32)
    y = y + jnp.dot(cv_ref[...], w_ref[D_ATT:D_ATT + D_CONV, :].astype(BF16), preferred_element_type=F32)
    y = y + jnp.dot(rec_ref[...], w_ref[D_ATT + D_CONV:D_MIX, :].astype(BF16), preferred_element_type=F32)
    o_ref[...] = x_ref[...] + gate_ref[...] * y


def _outproj(l, att, cv, rec, w_out, x, mod4):
    return pl.pallas_call(
        _outproj_kernel,
        out_shape=jax.ShapeDtypeStruct((T_ALL, D_MODEL), F32),
        grid_spec=pltpu.PrefetchScalarGridSpec(
            num_scalar_prefetch=1,
            grid=(N_MT, D_MODEL // TN_OUT),
            in_specs=[
                pl.BlockSpec((TM, D_ATT), lambda i, j, l: (i, 0)),
                pl.BlockSpec((TM, D_CONV), lambda i, j, l: (i, 0)),
                pl.BlockSpec((TM, D_LRU), lambda i, j, l: (i, 0)),
                pl.BlockSpec((None, D_MIX, TN_OUT), lambda i, j, l: (l[0], 0, j)),
                pl.BlockSpec((TM, TN_OUT), lambda i, j, l: (i, j)),
                pl.BlockSpec((None, None, 1, TN_OUT),
                             lambda i, j, l: (l[0], _mod_row(i), 0, 2 * (D_MODEL // TN_OUT) + j)),
            ],
            out_specs=pl.BlockSpec((TM, TN_OUT), lambda i, j, l: (i, j)),
        ),
        compiler_params=_cparams(2, 40),
        name="outproj",
    )(l, att, cv, rec, w_out, x, mod4)


TN_DOWN = 512


def _ffn_kernel(l_ref, x_ref, g_ref, sh_ref, sc_ref, gate_ref, wug_ref, wuv_ref, cwg_ref, cwv_ref,
                cbg_ref, cbv_ref, wd_ref, o_ref, h_scr, mprev_scr, mnext_scr):
    del l_ref
    i = pl.program_id(0)
    j = pl.program_id(1)

    @pl.when(j == 0)
    def _():
        _norm_mod(x_ref, g_ref, sc_ref, sh_ref, h_scr)
        o_ref[...] = jnp.zeros_like(o_ref)
        seq_len = jnp.where(i < CTX_MT, SEQ, DEC_SEQ)
        pos = lax.broadcasted_iota(jnp.int32, (TM, TF), 0) & (seq_len - 1)
        mprev_scr[...] = jnp.where(pos == 0, 0.0, 1.0)
        mnext_scr[...] = jnp.where(pos == seq_len - 1, 0.0, 1.0)

    h = h_scr[...]

    def branch(w_ref, cw_ref, cb_ref):
        u = jnp.dot(h, w_ref[...].astype(BF16), preferred_element_type=F32)
        prev = pltpu.roll(u, 1, 0) * mprev_scr[...]
        nxt = pltpu.roll(u, TM - 1, 0) * mnext_scr[...]
        return cw_ref[0:1, :] * prev + cw_ref[1:2, :] * u + cw_ref[2:3, :] * nxt + cb_ref[...]

    ug = branch(wug_ref, cwg_ref, cbg_ref)
    uv = branch(wuv_ref, cwv_ref, cbv_ref)
    act = (_silu(ug) * uv).astype(BF16)
    for n in range(D_MODEL // TN_DOWN):
        ns = slice(n * TN_DOWN, (n + 1) * TN_DOWN)
        o_ref[:, ns] = jnp.dot(act, wd_ref[:, ns].astype(BF16), preferred_element_type=F32) + o_ref[:, ns]

    @pl.when(j == N_FT - 1)
    def _():
        o_ref[...] = x_ref[...] + gate_ref[...] * o_ref[...]


def _ffn(l, x, ln_g, mod4, ffn_up, conv_w, conv_b, ffn_down):
    return pl.pallas_call(
        _ffn_kernel,
        out_shape=jax.ShapeDtypeStruct((T_ALL, D_MODEL), F32),
        grid_spec=pltpu.PrefetchScalarGridSpec(
            num_scalar_prefetch=1,
            grid=(N_MT, N_FT),
            in_specs=[
                pl.BlockSpec((TM, D_MODEL), lambda i, j, l: (i, 0), pipeline_mode=pl.Buffered(1)),
                pl.BlockSpec((None, 1, D_MODEL), lambda i, j, l: (l[0], 0, 0)),
                _mod_spec(3),
                _mod_spec(4),
                _mod_spec(5),
                pl.BlockSpec((None, D_MODEL, TF), lambda i, j, l: (l[0], 0, j)),
                pl.BlockSpec((None, D_MODEL, TF), lambda i, j, l: (l[0], 0, N_FT + j)),
                pl.BlockSpec((None, 3, TF), lambda i, j, l: (l[0], 0, j)),
                pl.BlockSpec((None, 3, TF), lambda i, j, l: (l[0], 0, N_FT + j)),
                pl.BlockSpec((None, 1, TF), lambda i, j, l: (l[0], 0, j)),
                pl.BlockSpec((None, 1, TF), lambda i, j, l: (l[0], 0, N_FT + j)),
                pl.BlockSpec((None, TF, D_MODEL), lambda i, j, l: (l[0], j, 0)),
            ],
            out_specs=pl.BlockSpec((TM, D_MODEL), lambda i, j, l: (i, 0)),
            scratch_shapes=[
                pltpu.VMEM((TM, D_MODEL), BF16),
                pltpu.VMEM((TM, TF), F32),
                pltpu.VMEM((TM, TF), F32),
            ],
        ),
        compiler_params=_cparams(2, 62),
        name="convffn",
    )(l, x, ln_g, mod4, mod4, mod4, ffn_up, ffn_up, conv_w, conv_w, conv_b, conv_b, ffn_down)


def _final_kernel(x_ref, g_ref, o_ref):
    g = g_ref[...]

    def body(c, carry):
        r0 = pl.multiple_of(c * NORM_ROWS, NORM_ROWS)
        x = x_ref[pl.ds(r0, NORM_ROWS), :]
        ms = jnp.mean(x * x, axis=-1, keepdims=True)
        o_ref[pl.ds(r0, NORM_ROWS), :] = x * lax.rsqrt(ms + EPS) * g
        return carry

    lax.fori_loop(0, TM // NORM_ROWS, body, 0)


def _final_norm(x, g, tile0, n_tiles):
    return pl.pallas_call(
        _final_kernel,
        out_shape=jax.ShapeDtypeStruct((n_tiles * TM, D_MODEL), F32),
        grid=(n_tiles,),
        in_specs=[
            pl.BlockSpec((TM, D_MODEL), lambda i: (tile0 + i, 0)),
            pl.BlockSpec((1, D_MODEL), lambda i: (0, 0)),
        ],
        out_specs=pl.BlockSpec((TM, D_MODEL), lambda i: (i, 0)),
        compiler_params=_cparams(1, 48),
        name="final_norm",
    )(x, g)


def _block_diag_halves(w):
    nb = LRU_BLOCKS // 2
    w6 = w.reshape(DEPTH, 2, 2, nb, LRU_BW, LRU_BW)
    eye = jnp.eye(nb, dtype=w.dtype)
    dense = jnp.einsum("ldhjab,jk->ldhjakb", w6, eye)
    return dense.reshape(DEPTH, 2, 2, LRU_HALF, LRU_HALF).astype(BF16)


def kernel(x_prompt, x_sample, cache_k, cache_v, state_lru, c, c_ctx, ln1_g, w_mod, b_mod, w_in, na_bias,
           cv_w, cv_b, cv_ln_g, cv_ln_b, lru_conv_w, lru_conv_b, lru_wa, lru_ba, lru_wi, lru_bi, lru_lam,
           w_out, ln2_g, ffn_up, ffn_conv_w, ffn_conv_b, ffn_down, final_g):
    x = jnp.concatenate([x_prompt.reshape(T_CTX, D_MODEL), x_sample.reshape(T_LAT, D_MODEL)], axis=0)
    cond = jnp.concatenate([c_ctx[None, :], c, jnp.zeros((N_COND - 1 - DEC_BATCH, D_MODEL), F32)], axis=0)
    mod4 = _modulation(cond, w_mod, b_mod).reshape(DEPTH, N_COND, 1, 6 * D_MODEL)
    tt = _bias_table(na_bias)
    ck = cache_k.reshape(DEC_BATCH, DEPTH, PAST_LEN, D_ATT)
    cvv = cache_v.reshape(DEC_BATCH, DEPTH, PAST_LEN, D_ATT)
    state = jnp.swapaxes(state_lru, 0, 1)
    h0_ctx = jnp.zeros((BATCH, 2, D_LRU), F32)
    lru_p = {
        "lru_conv_w": lru_conv_w,
        "lru_conv_b": lru_conv_b.reshape(DEPTH, 1, D_LRU),
        "wa_dense": _block_diag_halves(lru_wa),
        "lru_ba": lru_ba,
        "wi_dense": _block_diag_halves(lru_wi),
        "lru_bi": lru_bi,
        "lru_lam": lru_lam,
    }
    ln1 = ln1_g.reshape(DEPTH, 1, D_MODEL)
    ln2 = ln2_g.reshape(DEPTH, 1, D_MODEL)
    cvb = cv_b.reshape(DEPTH, 1, D_CONV)
    cvg = cv_ln_g.reshape(DEPTH, 1, D_CONV)
    cvlb = cv_ln_b.reshape(DEPTH, 1, D_CONV)
    fcb = ffn_conv_b.reshape(DEPTH, 1, 2 * D_FF)
    lat_sb = T_CTX // DEC_SEQ

    def layer(carry, xs):
        x, ks, vs, hs = carry
        li, h0_lat = xs
        l = jnp.reshape(li, (1,))
        z = _inproj(l, x, ln1, mod4, w_in)
        att = _att_ctx(z)
        att = _att_lat(l, z, ck, cvv, tt, att)
        cvo = _convmod(l, z, cv_w, cvb, cvg, cvlb, SEQ, BATCH, 0)
        cvo = _convmod(l, z, cv_w, cvb, cvg, cvlb, DEC_SEQ, DEC_BATCH, lat_sb, prev=cvo)
        rec, h_last = _lru(l, z, h0_ctx, lru_p, SEQ, BATCH, 0)
        rec, _ = _lru(l, z, h0_lat, lru_p, DEC_SEQ, DEC_BATCH, lat_sb, prev=rec)
        x = _outproj(l, att, cvo, rec, w_out, x, mod4)
        x = _ffn(l, x, ln2, mod4, ffn_up, ffn_conv_w, fcb, ffn_down)
        k_l = z[:T_CTX, D_ATT:2 * D_ATT].reshape(BATCH, 1, SEQ, D_ATT)
        v_l = z[:T_CTX, 2 * D_ATT:3 * D_ATT].reshape(BATCH, 1, SEQ, D_ATT)
        ks = lax.dynamic_update_slice(ks, k_l, (0, li, 0, 0))
        vs = lax.dynamic_update_slice(vs, v_l, (0, li, 0, 0))
        hs = lax.dynamic_update_slice(hs, h_last[:, None], (0, li, 0, 0))
        return (x, ks, vs, hs), None

    init = (x,
            jnp.zeros((BATCH, DEPTH, SEQ, D_ATT), F32),
            jnp.zeros((BATCH, DEPTH, SEQ, D_ATT), F32),
            jnp.zeros((BATCH, DEPTH, 2, D_LRU), F32))
    (x, ks, vs, hs), _ = lax.scan(layer, init, (jnp.arange(DEPTH, dtype=jnp.int32), state))

    g = final_g.reshape(1, D_MODEL)
    y_prompt = _final_norm(x, g, 0, CTX_MT).reshape(BATCH, SEQ, D_MODEL)
    y_sample = _final_norm(x, g, CTX_MT, N_MT - CTX_MT).reshape(DEC_BATCH, DEC_SEQ, D_MODEL)
    new_k = ks.reshape(BATCH, DEPTH, SEQ, N_HEADS, HEAD_DIM)
    new_v = vs.reshape(BATCH, DEPTH, SEQ, N_HEADS, HEAD_DIM)
    return (y_prompt, y_sample, new_k, new_v, hs)
```

```python
import functools

import jax
import jax.numpy as jnp
from jax import lax
from jax.experimental import pallas as pl
from jax.experimental.pallas import tpu as pltpu

F32 = jnp.float32
BF16 = jnp.bfloat16

D_MODEL = 2048
BATCH = 16
SEQ = 256
DEPTH = 4
DEC_BATCH = 2
DEC_SEQ = 1024
PAST_LEN = 512
GRID_W = 64
GRID_R = DEC_SEQ // GRID_W
D_ATT = 1024
N_HEADS = 8
HEAD_DIM = 128
D_CONV = 512
D_LRU = 512
D_MIX = D_ATT + D_CONV + D_LRU
D_IN = 3 * D_ATT + 2 * D_CONV + 2 * D_LRU
WIN_ROWS = 8
WIN_COLS = 16
CONV_WIDTH = 31
LRU_CONV_WIDTH = 4
LRU_BLOCKS = 8
LRU_BW = D_LRU // LRU_BLOCKS
LRU_C = 8.0
D_FF = 5632
EPS = 1e-6
ATT_SCALE = HEAD_DIM ** -0.5
NEG_INF = -1e30

T_CTX = BATCH * SEQ
T_LAT = DEC_BATCH * DEC_SEQ
T_ALL = T_CTX + T_LAT

V7X_VMEM_BYTES = 64 * 1024 * 1024
SUBLANES = 8
LANES = 128

TM = 1024
N_MT = T_ALL // TM
CTX_MT = T_CTX // TM
TM2 = 2 * TM
N_MT2 = T_ALL // TM2
TM_FINAL = 256
TN_IN = 512
TN_OUT = 512
TF = 512
N_FT = D_FF // TF
TN_MOD = 1024
NORM_ROWS = 128
N_COND = 8
BIAS_ROWS = 2 * WIN_ROWS - 1
BIAS_COLS = 2 * WIN_COLS - 1


def _cparams(n_axes, vmem_mib, flags=None):
    return pltpu.CompilerParams(
        dimension_semantics=("arbitrary",) * n_axes,
        vmem_limit_bytes=min(vmem_mib * 1024 * 1024, V7X_VMEM_BYTES - 2 * 1024 * 1024),
        flags=flags,
    )


def _sigmoid(x):
    return 0.5 * jnp.tanh(0.5 * x) + 0.5


def _silu(x):
    return x * _sigmoid(x)


def _gelu_tanh(x):
    return 0.5 * x * (1.0 + jnp.tanh(0.7978845608028654 * (x + 0.044715 * (x * x * x))))


def _mod_row(i):
    return jnp.maximum(i - (CTX_MT - 1), 0)


def _mod_kernel(c_ref, w_ref, b_ref, o_ref):
    a = _silu(c_ref[...]).astype(BF16)
    o_ref[...] = jnp.dot(a, w_ref[...].astype(BF16), preferred_element_type=F32) + b_ref[...]


def _modulation(cond, w_mod, b_mod):
    n_out = w_mod.shape[-1]
    return pl.pallas_call(
        _mod_kernel,
        out_shape=jax.ShapeDtypeStruct((DEPTH, N_COND, n_out), F32),
        grid=(DEPTH, n_out // TN_MOD),
        in_specs=[
            pl.BlockSpec((N_COND, D_MODEL), lambda l, j: (0, 0)),
            pl.BlockSpec((None, D_MODEL, TN_MOD), lambda l, j: (l, 0, j)),
            pl.BlockSpec((None, 1, TN_MOD), lambda l, j: (l, 0, j)),
        ],
        out_specs=pl.BlockSpec((None, N_COND, TN_MOD), lambda l, j: (l, 0, j)),
        compiler_params=_cparams(2, 40),
        name="modulation",
    )(cond, w_mod, b_mod.reshape(DEPTH, 1, n_out))


def _bias_kernel(rb_ref, o_ref, td_scr):
    base = (pl.program_id(0) * N_HEADS + pl.program_id(1)) * (BIAS_ROWS * BIAS_COLS)
    qi = lax.broadcasted_iota(jnp.int32, (GRID_W, LANES), 0)
    li = lax.broadcasted_iota(jnp.int32, (GRID_W, LANES), 1)
    diff = jnp.clip((li & (GRID_W - 1)) - qi, -(WIN_COLS - 1), WIN_COLS - 1) + (WIN_COLS - 1)

    def body(i, carry):
        acc = jnp.zeros((GRID_W, LANES), F32)
        for d in range(BIAS_COLS):
            acc = jnp.where(diff == d, rb_ref[base + i * BIAS_COLS + d], acc)
        td_scr[i] = acc
        return carry

    lax.fori_loop(0, BIAS_ROWS, body, 0)
    td_scr[BIAS_ROWS] = jnp.zeros((GRID_W, LANES), F32)
    left = li < GRID_W
    for i in range(BIAS_ROWS):
        o_ref[i] = jnp.where(left, td_scr[i], td_scr[i + 1])


def _bias_table(na_bias):
    return pl.pallas_call(
        _bias_kernel,
        out_shape=jax.ShapeDtypeStruct((DEPTH, N_HEADS, BIAS_ROWS, GRID_W, LANES), F32),
        grid=(DEPTH, N_HEADS),
        in_specs=[pl.BlockSpec(memory_space=pltpu.SMEM)],
        out_specs=pl.BlockSpec((None, None, BIAS_ROWS, GRID_W, LANES), lambda l, h: (l, h, 0, 0, 0)),
        scratch_shapes=[pltpu.VMEM((BIAS_ROWS + 1, GRID_W, LANES), F32)],
        compiler_params=_cparams(2, 16),
        name="bias_table",
    )(na_bias.reshape(-1))


def _norm_mod(x_ref, g_ref, sc_ref, sh_ref, h_scr, row0=0):
    g = g_ref[...]
    scale = 1.0 + sc_ref[...]
    shift = sh_ref[...]

    def body(c, carry):
        r0 = pl.multiple_of(row0 + c * NORM_ROWS, NORM_ROWS)
        x = x_ref[pl.ds(r0, NORM_ROWS), :]
        ms = jnp.mean(x * x, axis=-1, keepdims=True)
        y = x * lax.rsqrt(ms + EPS) * g
        h_scr[pl.ds(r0, NORM_ROWS), :] = (y * scale + shift).astype(BF16)
        return carry

    lax.fori_loop(0, TM // NORM_ROWS, body, 0)


def _mod_spec(chunk):
    return pl.BlockSpec((None, None, 1, D_MODEL), lambda i, j, l: (l[0], _mod_row(i), 0, chunk))


def _mod_spec2(chunk, half):
    return pl.BlockSpec((None, None, 1, D_MODEL), lambda i, j, l: (l[0], _mod_row(2 * i + half), 0, chunk))


def _inproj_kernel(l_ref, x_ref, g_ref, sh0_ref, sc0_ref, sh1_ref, sc1_ref, w_ref, o_ref, h_scr):
    @pl.when(pl.program_id(1) == 0)
    def _():
        _norm_mod(x_ref, g_ref, sc0_ref, sh0_ref, h_scr, 0)
        _norm_mod(x_ref, g_ref, sc1_ref, sh1_ref, h_scr, TM)

    o_ref[...] = jnp.dot(h_scr[...], w_ref[...], preferred_element_type=F32)


def _inproj(l, x, ln_g, mod4, w_in):
    return pl.pallas_call(
        _inproj_kernel,
        out_shape=jax.ShapeDtypeStruct((T_ALL, D_IN), F32),
        grid_spec=pltpu.PrefetchScalarGridSpec(
            num_scalar_prefetch=1,
            grid=(N_MT2, D_IN // TN_IN),
            in_specs=[
                pl.BlockSpec((TM2, D_MODEL), lambda i, j, l: (i, 0), pipeline_mode=pl.Buffered(1)),
                pl.BlockSpec((None, 1, D_MODEL), lambda i, j, l: (l[0], 0, 0)),
                _mod_spec2(0, 0),
                _mod_spec2(1, 0),
                _mod_spec2(0, 1),
                _mod_spec2(1, 1),
                pl.BlockSpec((None, D_MODEL, TN_IN), lambda i, j, l: (l[0], 0, j)),
            ],
            out_specs=pl.BlockSpec((TM2, TN_IN), lambda i, j, l: (i, j)),
            scratch_shapes=[pltpu.VMEM((TM2, D_MODEL), BF16)],
        ),
        compiler_params=_cparams(2, 48),
        name="inproj",
    )(l, x, ln_g, mod4, mod4, mod4, mod4, w_in)


def _dot_nt(a, b):
    return lax.dot_general(a, b, (((1,), (1,)), ((), ())), preferred_element_type=F32)


def _att_ctx_kernel(l_ref, q_ref, k_ref, v_ref, ks_in_ref, vs_in_ref, o_ref, nk_ref, nv_ref):
    del l_ref, ks_in_ref, vs_in_ref
    for h in range(N_HEADS):
        sl = slice(h * HEAD_DIM, (h + 1) * HEAD_DIM)
        kf = k_ref[:, sl]
        vf = v_ref[:, sl]
        nk_ref[pl.ds(h, SEQ, stride=N_HEADS), :] = kf
        nv_ref[pl.ds(h, SEQ, stride=N_HEADS), :] = vf
        q = q_ref[:, sl].astype(BF16)
        k = kf.astype(BF16)
        v = vf.astype(BF16)
        s = _dot_nt(q, k) * ATT_SCALE
        p = jnp.exp(s - jnp.max(s, axis=-1, keepdims=True))
        denom = jnp.sum(p, axis=-1, keepdims=True)
        o = jnp.dot(p.astype(BF16), v, preferred_element_type=F32)
        o_ref[:, sl] = (o / denom).astype(o_ref.dtype)


def _att_ctx(l, z, ks, vs):
    kv_shape = jax.ShapeDtypeStruct((BATCH, DEPTH, SEQ * N_HEADS, HEAD_DIM), F32)
    kv_spec = pl.BlockSpec((None, None, SEQ * N_HEADS, HEAD_DIM), lambda b, l: (b, l[0], 0, 0))
    return pl.pallas_call(
        _att_ctx_kernel,
        out_shape=(jax.ShapeDtypeStruct((T_ALL, D_ATT), BF16), kv_shape, kv_shape),
        grid_spec=pltpu.PrefetchScalarGridSpec(
            num_scalar_prefetch=1,
            grid=(BATCH,),
            in_specs=[
                pl.BlockSpec((SEQ, D_ATT), lambda b, l: (b, 0)),
                pl.BlockSpec((SEQ, D_ATT), lambda b, l: (b, 1)),
                pl.BlockSpec((SEQ, D_ATT), lambda b, l: (b, 2)),
                pl.BlockSpec(memory_space=pl.ANY),
                pl.BlockSpec(memory_space=pl.ANY),
            ],
            out_specs=(pl.BlockSpec((SEQ, D_ATT), lambda b, l: (b, 0)), kv_spec, kv_spec),
        ),
        input_output_aliases={4: 1, 5: 2},
        compiler_params=_cparams(1, 32),
        name="att_ctx",
    )(l, z, z, z, ks, vs)


Q_ROWS = 4
_KEY_ROWS = ((0, 8), (0, 12), (4, 16), (8, 16))


def _win_start(rq):
    return min(max(rq - WIN_ROWS // 2, 0), GRID_R - WIN_ROWS)


def _att_lat_kernel(l_ref, q_ref, k_ref, v_ref, ck_ref, cv_ref, tt_ref, att_in_ref, o_ref):
    del l_ref, att_in_ref
    qi = lax.broadcasted_iota(jnp.int32, (GRID_W, LANES), 0)
    li = lax.broadcasted_iota(jnp.int32, (GRID_W, LANES), 1)
    kc = li & (GRID_W - 1)
    cs = jnp.clip(qi - WIN_COLS // 2, 0, GRID_W - WIN_COLS)
    in_cols = (kc >= cs) & (kc < cs + WIN_COLS)
    left = li < GRID_W
    mask_both = in_cols
    mask_left = in_cols & left
    mask_right = in_cols & jnp.logical_not(left)
    neg = jnp.full((GRID_W, LANES), NEG_INF, F32)

    kb = k_ref[...].astype(BF16)
    vb = v_ref[...].astype(BF16)
    head = pl.program_id(1)
    ckb = ck_ref[pl.ds(head, PAST_LEN, stride=N_HEADS), :].astype(BF16)
    cvb = cv_ref[pl.ds(head, PAST_LEN, stride=N_HEADS), :].astype(BF16)
    tq = Q_ROWS * GRID_W
    for t in range(GRID_R // Q_ROWS):
        lo, hi = _KEY_ROWS[t]
        q = q_ref[t * tq:(t + 1) * tq, :].astype(BF16)
        s_w = _dot_nt(q, kb[lo * GRID_W:hi * GRID_W, :])
        s_c = _dot_nt(q, ckb) * ATT_SCALE
        rows = []
        for rl in range(Q_ROWS):
            rq = t * Q_ROWS + rl
            start = _win_start(rq)
            blks = []
            for m in range(lo // 2, hi // 2):
                v0 = start <= 2 * m < start + WIN_ROWS
                v1 = start <= 2 * m + 1 < start + WIN_ROWS
                if not (v0 or v1):
                    blks.append(neg)
                    continue
                i = 2 * m - rq + WIN_ROWS - 1
                assert 0 <= i < BIAS_ROWS
                mask = mask_both if (v0 and v1) else (mask_left if v0 else mask_right)
                c0 = (2 * m - lo) * GRID_W
                sb = s_w[rl * GRID_W:(rl + 1) * GRID_W, c0:c0 + LANES]
                blks.append(jnp.where(mask, sb * ATT_SCALE + tt_ref[i], NEG_INF))
            rows.append(jnp.concatenate(blks, axis=1))
        sw = jnp.concatenate(rows, axis=0)
        mx = jnp.maximum(jnp.max(sw, axis=-1, keepdims=True), jnp.max(s_c, axis=-1, keepdims=True))
        pw = jnp.exp(sw - mx)
        pc = jnp.exp(s_c - mx)
        denom = jnp.sum(pw, axis=-1, keepdims=True) + jnp.sum(pc, axis=-1, keepdims=True)
        o = jnp.dot(pw.astype(BF16), vb[lo * GRID_W:hi * GRID_W, :], preferred_element_type=F32)
        o = o + jnp.dot(pc.astype(BF16), cvb, preferred_element_type=F32)
        o_ref[t * tq:(t + 1) * tq, :] = (o / denom).astype(o_ref.dtype)


def _att_lat(l, z, cache_k, cache_v, tt, att):
    rb0 = T_CTX // DEC_SEQ
    hb = D_ATT // HEAD_DIM
    return pl.pallas_call(
        _att_lat_kernel,
        out_shape=jax.ShapeDtypeStruct((T_ALL, D_ATT), BF16),
        grid_spec=pltpu.PrefetchScalarGridSpec(
            num_scalar_prefetch=1,
            grid=(DEC_BATCH, N_HEADS),
            in_specs=[
                pl.BlockSpec((DEC_SEQ, HEAD_DIM), lambda b, h, l: (rb0 + b, h)),
                pl.BlockSpec((DEC_SEQ, HEAD_DIM), lambda b, h, l: (rb0 + b, hb + h)),
                pl.BlockSpec((DEC_SEQ, HEAD_DIM), lambda b, h, l: (rb0 + b, 2 * hb + h)),
                pl.BlockSpec((None, None, PAST_LEN * N_HEADS, HEAD_DIM), lambda b, h, l: (b, l[0], 0, 0)),
                pl.BlockSpec((None, None, PAST_LEN * N_HEADS, HEAD_DIM), lambda b, h, l: (b, l[0], 0, 0)),
                pl.BlockSpec((None, None, BIAS_ROWS, GRID_W, LANES), lambda b, h, l: (l[0], h, 0, 0, 0)),
                pl.BlockSpec(memory_space=pl.ANY),
            ],
            out_specs=pl.BlockSpec((DEC_SEQ, HEAD_DIM), lambda b, h, l: (rb0 + b, h)),
        ),
        input_output_aliases={7: 0},
        compiler_params=_cparams(2, 32),
        name="att_lat",
    )(l, z, z, z, cache_k, cache_v, tt, att)


CV_PAD = 16
CV_ROWS = 64


def _convmod_kernel(l_ref, a_ref, g_ref, w_ref, b_ref, lng_ref, lnb_ref, *rest):
    del l_ref
    o_ref, upad, ush = rest[-3:]
    s_len = a_ref.shape[0]
    n_sh = s_len + CV_PAD + SUBLANES
    upad[0:CV_PAD, :] = jnp.zeros((CV_PAD, D_CONV), F32)
    upad[CV_PAD + s_len:, :] = jnp.zeros((CV_PAD, D_CONV), F32)
    upad[CV_PAD:CV_PAD + s_len, :] = a_ref[...] * _sigmoid(g_ref[...])
    for j in range(SUBLANES):
        ush[j, 0:n_sh, :] = upad[j:j + n_sh, :]
    bias = b_ref[...]
    ln_g = lng_ref[...]
    ln_b = lnb_ref[...]

    def chunk(c, carry):
        r0 = pl.multiple_of(c * CV_ROWS, CV_ROWS)
        acc = jnp.broadcast_to(bias, (CV_ROWS, D_CONV))
        for k in range(CONV_WIDTH):
            m, j = divmod(k + CV_PAD - CONV_WIDTH // 2, SUBLANES)
            acc = acc + w_ref[k:k + 1, :] * ush[j, pl.ds(r0 + SUBLANES * m, CV_ROWS), :]
        mu = jnp.mean(acc, axis=-1, keepdims=True)
        xc = acc - mu
        var = jnp.mean(xc * xc, axis=-1, keepdims=True)
        y = xc * lax.rsqrt(var + EPS) * ln_g + ln_b
        o_ref[pl.ds(r0, CV_ROWS), :] = _silu(y).astype(o_ref.dtype)
        return carry

    lax.fori_loop(0, s_len // CV_ROWS, chunk, 0)


def _convmod(l, z, cv_w, cv_b, ln_g, ln_b, s_len, n_seq, row_block0, prev=None):
    cb = (3 * D_ATT) // D_CONV
    in_specs = [
        pl.BlockSpec((s_len, D_CONV), lambda b, l: (row_block0 + b, cb)),
        pl.BlockSpec((s_len, D_CONV), lambda b, l: (row_block0 + b, cb + 1)),
        pl.BlockSpec((None, CONV_WIDTH, D_CONV), lambda b, l: (l[0], 0, 0)),
        pl.BlockSpec((None, 1, D_CONV), lambda b, l: (l[0], 0, 0)),
        pl.BlockSpec((None, 1, D_CONV), lambda b, l: (l[0], 0, 0)),
        pl.BlockSpec((None, 1, D_CONV), lambda b, l: (l[0], 0, 0)),
    ]
    args = [l, z, z, cv_w, cv_b, ln_g, ln_b]
    aliases = {}
    if prev is not None:
        in_specs.append(pl.BlockSpec(memory_space=pl.ANY))
        args.append(prev)
        aliases = {len(args) - 1: 0}
    return pl.pallas_call(
        _convmod_kernel,
        out_shape=jax.ShapeDtypeStruct((T_ALL, D_CONV), BF16),
        grid_spec=pltpu.PrefetchScalarGridSpec(
            num_scalar_prefetch=1,
            grid=(n_seq,),
            in_specs=in_specs,
            out_specs=pl.BlockSpec((s_len, D_CONV), lambda b, l: (row_block0 + b, 0)),
            scratch_shapes=[
                pltpu.VMEM((s_len + 2 * CV_PAD, D_CONV), F32),
                pltpu.VMEM((SUBLANES, s_len + CV_PAD + SUBLANES, D_CONV), F32),
            ],
        ),
        input_output_aliases=aliases,
        compiler_params=_cparams(1, 48),
        name=f"convmod_{s_len}",
    )(*args)


LRU_PAD = 8
LRU_HALF = D_LRU // 2
LRU_CHUNK = 256


def _softplus(x):
    return jnp.maximum(x, 0.0) + jnp.log1p(jnp.exp(-jnp.abs(x)))


def _lru_kernel(l_ref, x_ref, g_ref, h0_ref, cw_ref, cb_ref, wa_ref, ba_ref, wi_ref, bi_ref, lam_ref, *rest):
    del l_ref
    y_ref, hl_ref, xpad, a_scr, b_scr, h_scr = rest[-6:]
    s_len = x_ref.shape[0]
    n_blk = s_len // SUBLANES
    n_lt = D_LRU // LANES

    xpad[0:LRU_PAD, :] = jnp.zeros((LRU_PAD, D_LRU), F32)
    xpad[LRU_PAD + s_len:, :] = jnp.zeros((LRU_PAD, D_LRU), F32)
    xpad[LRU_PAD:LRU_PAD + s_len, :] = x_ref[...]
    cb = cb_ref[...]
    decay = [-LRU_C * _softplus(-lam_ref[d:d + 1, :]) for d in range(2)]

    for c in range(s_len // LRU_CHUNK):
        r0 = c * LRU_CHUNK
        xc = jnp.broadcast_to(cb, (LRU_CHUNK, D_LRU))
        for k in range(LRU_CONV_WIDTH):
            off = LRU_PAD + r0 + k - LRU_CONV_WIDTH // 2
            xc = xc + cw_ref[k:k + 1, :] * xpad[off:off + LRU_CHUNK, :]
        xb = xc.astype(BF16)
        for d in range(2):
            ga = jnp.concatenate(
                [jnp.dot(xb[:, hh * LRU_HALF:(hh + 1) * LRU_HALF], wa_ref[d, hh], preferred_element_type=F32)
                 for hh in range(2)], axis=1) + ba_ref[d:d + 1, :]
            gi = jnp.concatenate(
                [jnp.dot(xb[:, hh * LRU_HALF:(hh + 1) * LRU_HALF], wi_ref[d, hh], preferred_element_type=F32)
                 for hh in range(2)], axis=1) + bi_ref[d:d + 1, :]
            log_a = _sigmoid(ga) * decay[d]
            th = jnp.tanh(log_a)
            gain = jnp.sqrt(-2.0 * th / (1.0 - th))
            a_scr[d, r0:r0 + LRU_CHUNK, :] = jnp.exp(log_a)
            b_scr[d, r0:r0 + LRU_CHUNK, :] = gain * (_sigmoid(gi) * xc)

    row = lax.broadcasted_iota(jnp.int32, (SUBLANES, LANES), 0)

    def scan_block(it, carry):
        new = []
        for d in range(2):
            blk = it if d == 0 else n_blk - 1 - it
            r0 = pl.multiple_of(blk * SUBLANES, SUBLANES)
            for c in range(n_lt):
                ls = slice(c * LANES, (c + 1) * LANES)
                a = a_scr[d, pl.ds(r0, SUBLANES), ls]
                b = b_scr[d, pl.ds(r0, SUBLANES), ls]
                for s in (1, 2, 4):
                    shift = s if d == 0 else SUBLANES - s
                    valid = (row >= s) if d == 0 else (row < SUBLANES - s)
                    a_sh = jnp.where(valid, pltpu.roll(a, shift, 0), 1.0)
                    b_sh = jnp.where(valid, pltpu.roll(b, shift, 0), 0.0)
                    b = a * b_sh + b
                    a = a * a_sh
                h = a * carry[d * n_lt + c] + b
                h_scr[d, pl.ds(r0, SUBLANES), ls] = h
                last = h[SUBLANES - 1:SUBLANES, :] if d == 0 else h[0:1, :]
                new.append(jnp.broadcast_to(last, (SUBLANES, LANES)))
        return tuple(new)

    init = tuple(
        jnp.broadcast_to(h0_ref[d:d + 1, c * LANES:(c + 1) * LANES], (SUBLANES, LANES))
        for d in range(2) for c in range(n_lt))
    final = lax.fori_loop(0, n_blk, scan_block, init)
    for d in range(2):
        for c in range(n_lt):
            hl_ref[d:d + 1, c * LANES:(c + 1) * LANES] = final[d * n_lt + c][0:1, :]

    for c in range(s_len // LRU_CHUNK):
        rs = slice(c * LRU_CHUNK, (c + 1) * LRU_CHUNK)
        y = (h_scr[0, rs, :] + h_scr[1, rs, :]) * _gelu_tanh(g_ref[rs, :])
        y_ref[rs, :] = y.astype(y_ref.dtype)


def _lru(l, z, h0, p, s_len, n_seq, row_block0, prev=None):
    cb = (3 * D_ATT + 2 * D_CONV) // D_LRU
    in_specs = [
        pl.BlockSpec((s_len, D_LRU), lambda b, l: (row_block0 + b, cb)),
        pl.BlockSpec((s_len, D_LRU), lambda b, l: (row_block0 + b, cb + 1)),
        pl.BlockSpec((None, 2, D_LRU), lambda b, l: (b, 0, 0)),
        pl.BlockSpec((None, LRU_CONV_WIDTH, D_LRU), lambda b, l: (l[0], 0, 0)),
        pl.BlockSpec((None, 1, D_LRU), lambda b, l: (l[0], 0, 0)),
        pl.BlockSpec((None, 2, 2, LRU_HALF, LRU_HALF), lambda b, l: (l[0], 0, 0, 0, 0)),
        pl.BlockSpec((None, 2, D_LRU), lambda b, l: (l[0], 0, 0)),
        pl.BlockSpec((None, 2, 2, LRU_HALF, LRU_HALF), lambda b, l: (l[0], 0, 0, 0, 0)),
        pl.BlockSpec((None, 2, D_LRU), lambda b, l: (l[0], 0, 0)),
        pl.BlockSpec((None, 2, D_LRU), lambda b, l: (l[0], 0, 0)),
    ]
    args = [l, z, z, h0, p["lru_conv_w"], p["lru_conv_b"], p["wa_dense"], p["lru_ba"], p["wi_dense"],
            p["lru_bi"], p["lru_lam"]]
    aliases = {}
    if prev is not None:
        in_specs.append(pl.BlockSpec(memory_space=pl.ANY))
        args.append(prev)
        aliases = {len(args) - 1: 0}
    return pl.pallas_call(
        _lru_kernel,
        out_shape=(jax.ShapeDtypeStruct((T_ALL, D_LRU), BF16),
                   jax.ShapeDtypeStruct((n_seq, 2, D_LRU), F32)),
        grid_spec=pltpu.PrefetchScalarGridSpec(
            num_scalar_prefetch=1,
            grid=(n_seq,),
            in_specs=in_specs,
            out_specs=(pl.BlockSpec((s_len, D_LRU), lambda b, l: (row_block0 + b, 0)),
                       pl.BlockSpec((None, 2, D_LRU), lambda b, l: (b, 0, 0))),
            scratch_shapes=[
                pltpu.VMEM((s_len + 2 * LRU_PAD, D_LRU), F32),
                pltpu.VMEM((2, s_len, D_LRU), F32),
                pltpu.VMEM((2, s_len, D_LRU), F32),
                pltpu.VMEM((2, s_len, D_LRU), F32),
            ],
        ),
        input_output_aliases=aliases,
        compiler_params=_cparams(1, 48),
        name=f"rglru_{s_len}",
    )(*args)


def _outproj_kernel(l_ref, att_ref, cv_ref, rec_ref, w_ref, x_ref, gate0_ref, gate1_ref, o_ref):
    del l_ref
    y = jnp.dot(att_ref[...], w_ref[0:D_ATT, :], preferred_element_type=F32)
    y = y + jnp.dot(cv_ref[...], w_ref[D_ATT:D_ATT + D_CONV, :], preferred_element_type=F32)
    y = y + jnp.dot(rec_ref[...], w_ref[D_ATT + D_CONV:D_MIX, :], preferred_element_type=F32)
    o_ref[0:TM, :] = x_ref[0:TM, :] + gate0_ref[...] * y[0:TM, :]
    o_ref[TM:TM2, :] = x_ref[TM:TM2, :] + gate1_ref[...] * y[TM:TM2, :]


def _outproj(l, att, cv, rec, w_out, x, mod4):
    def gate_spec(half):
        return pl.BlockSpec((None, None, 1, TN_OUT),
                            lambda i, j, l: (l[0], _mod_row(2 * i + half), 0, 2 * (D_MODEL // TN_OUT) + j))

    return pl.pallas_call(
        _outproj_kernel,
        out_shape=jax.ShapeDtypeStruct((T_ALL, D_MODEL), F32),
        grid_spec=pltpu.PrefetchScalarGridSpec(
            num_scalar_prefetch=1,
            grid=(N_MT2, D_MODEL // TN_OUT),
            in_specs=[
                pl.BlockSpec((TM2, D_ATT), lambda i, j, l: (i, 0)),
                pl.BlockSpec((TM2, D_CONV), lambda i, j, l: (i, 0)),
                pl.BlockSpec((TM2, D_LRU), lambda i, j, l: (i, 0)),
                pl.BlockSpec((None, D_MIX, TN_OUT), lambda i, j, l: (l[0], 0, j)),
                pl.BlockSpec((TM2, TN_OUT), lambda i, j, l: (i, j)),
                gate_spec(0),
                gate_spec(1),
            ],
            out_specs=pl.BlockSpec((TM2, TN_OUT), lambda i, j, l: (i, j)),
        ),
        compiler_params=_cparams(2, 48),
        name="outproj",
    )(l, att, cv, rec, w_out, x, mod4, mod4)


TN_DOWN = 512
FFN_SPLIT = 4
FFN_HALO = 16
FFN_AHEAD = 3


def _ffn_kernel(l_ref, x_ref, g_ref, sh_ref, sc_ref, gate_ref, wug_ref, wuv_ref, cwg_ref, cwv_ref,
                cbg_ref, cbv_ref, wd_ref, o_ref, h_scr, mprev_scr, mnext_scr):
    del l_ref
    i = pl.program_id(0)
    j = pl.program_id(1)

    @pl.when(j == 0)
    def _():
        _norm_mod(x_ref, g_ref, sc_ref, sh_ref, h_scr)
        o_ref[...] = jnp.zeros_like(o_ref)
        seq_len = jnp.where(i < CTX_MT, SEQ, DEC_SEQ)
        pos = lax.broadcasted_iota(jnp.int32, (TM, LANES), 0) & (seq_len - 1)
        mprev_scr[...] = jnp.where(pos == 0, 0.0, 1.0)
        mnext_scr[...] = jnp.where(pos == seq_len - 1, 0.0, 1.0)

    rows = TM // FFN_SPLIT
    def bounds(s):
        return max(s * rows - FFN_HALO, 0), min((s + 1) * rows + FFN_HALO, TM)

    def up(s):
        lo, hi = bounds(s)
        h = h_scr[lo:hi, :]
        return (jnp.dot(h, wug_ref[...], preferred_element_type=F32),
                jnp.dot(h, wuv_ref[...], preferred_element_type=F32))

    def conv(u, s, cw_ref, cb_ref):
        lo, hi = bounds(s)
        off = s * rows - lo
        prev = pltpu.roll(u, 1, 0) * jnp.tile(mprev_scr[lo:hi, :], (1, TF // LANES))
        nxt = pltpu.roll(u, hi - lo - 1, 0) * jnp.tile(mnext_scr[lo:hi, :], (1, TF // LANES))
        c = cw_ref[0:1, :] * prev + cw_ref[1:2, :] * u + cw_ref[2:3, :] * nxt + cb_ref[...]
        return c[off:off + rows, :]

    pending = [up(s) for s in range(min(FFN_AHEAD, FFN_SPLIT))]
    for s in range(FFN_SPLIT):
        if s + FFN_AHEAD < FFN_SPLIT:
            pending.append(up(s + FFN_AHEAD))
        u_cur = pending.pop(0)
        ug = conv(u_cur[0], s, cwg_ref, cbg_ref)
        uv = conv(u_cur[1], s, cwv_ref, cbv_ref)
        act = (_silu(ug) * uv).astype(BF16)
        rs = slice(s * rows, (s + 1) * rows)
        for n in range(D_MODEL // TN_DOWN):
            ns = slice(n * TN_DOWN, (n + 1) * TN_DOWN)
            o_ref[rs, ns] = jnp.dot(act, wd_ref[:, ns], preferred_element_type=F32) + o_ref[rs, ns]

    @pl.when(j == N_FT - 1)
    def _():
        o_ref[...] = x_ref[...] + gate_ref[...] * o_ref[...]


def _ffn(l, x, ln_g, mod4, ffn_up, conv_w, conv_b, ffn_down):
    return pl.pallas_call(
        _ffn_kernel,
        out_shape=jax.ShapeDtypeStruct((T_ALL, D_MODEL), F32),
        grid_spec=pltpu.PrefetchScalarGridSpec(
            num_scalar_prefetch=1,
            grid=(N_MT, N_FT),
            in_specs=[
                pl.BlockSpec((TM, D_MODEL), lambda i, j, l: (i, 0), pipeline_mode=pl.Buffered(1)),
                pl.BlockSpec((None, 1, D_MODEL), lambda i, j, l: (l[0], 0, 0)),
                _mod_spec(3),
                _mod_spec(4),
                _mod_spec(5),
                pl.BlockSpec((None, D_MODEL, TF), lambda i, j, l: (l[0], 0, j)),
                pl.BlockSpec((None, D_MODEL, TF), lambda i, j, l: (l[0], 0, N_FT + j)),
                pl.BlockSpec((None, 3, TF), lambda i, j, l: (l[0], 0, j)),
                pl.BlockSpec((None, 3, TF), lambda i, j, l: (l[0], 0, N_FT + j)),
                pl.BlockSpec((None, 1, TF), lambda i, j, l: (l[0], 0, j)),
                pl.BlockSpec((None, 1, TF), lambda i, j, l: (l[0], 0, N_FT + j)),
                pl.BlockSpec((None, TF, D_MODEL), lambda i, j, l: (l[0], j, 0)),
            ],
            out_specs=pl.BlockSpec((TM, D_MODEL), lambda i, j, l: (i, 0)),
            scratch_shapes=[
                pltpu.VMEM((TM, D_MODEL), BF16),
                pltpu.VMEM((TM, LANES), F32),
                pltpu.VMEM((TM, LANES), F32),
            ],
        ),
        compiler_params=_cparams(2, 62),
        name="convffn",
    )(l, x, ln_g, mod4, mod4, mod4, ffn_up, ffn_up, conv_w, conv_w, conv_b, conv_b, ffn_down)


def _final_kernel(x_ref, g_ref, o_ref):
    g = g_ref[...]
    for c in range(TM_FINAL // NORM_ROWS):
        rs = slice(c * NORM_ROWS, (c + 1) * NORM_ROWS)
        x = x_ref[rs, :]
        ms = jnp.mean(x * x, axis=-1, keepdims=True)
        o_ref[rs, :] = x * lax.rsqrt(ms + EPS) * g


def _final_norm(x, g, row0, n_rows):
    tile0 = row0 // TM_FINAL
    return pl.pallas_call(
        _final_kernel,
        out_shape=jax.ShapeDtypeStruct((n_rows, D_MODEL), F32),
        grid=(n_rows // TM_FINAL,),
        in_specs=[
            pl.BlockSpec((TM_FINAL, D_MODEL), lambda i: (tile0 + i, 0)),
            pl.BlockSpec((1, D_MODEL), lambda i: (0, 0)),
        ],
        out_specs=pl.BlockSpec((TM_FINAL, D_MODEL), lambda i: (i, 0)),
        compiler_params=_cparams(1, 32),
        name="final_norm",
    )(x, g)


def _block_diag_halves(w):
    nb = LRU_BLOCKS // 2
    w6 = w.reshape(DEPTH, 2, 2, nb, LRU_BW, LRU_BW)
    eye = jnp.eye(nb, dtype=w.dtype)
    dense = jnp.einsum("ldhjab,jk->ldhjakb", w6, eye)
    return dense.reshape(DEPTH, 2, 2, LRU_HALF, LRU_HALF).astype(BF16)


def kernel(x_prompt, x_sample, cache_k, cache_v, state_lru, c, c_ctx, ln1_g, w_mod, b_mod, w_in, na_bias,
           cv_w, cv_b, cv_ln_g, cv_ln_b, lru_conv_w, lru_conv_b, lru_wa, lru_ba, lru_wi, lru_bi, lru_lam,
           w_out, ln2_g, ffn_up, ffn_conv_w, ffn_conv_b, ffn_down, final_g):
    x = jnp.concatenate([x_prompt.reshape(T_CTX, D_MODEL), x_sample.reshape(T_LAT, D_MODEL)], axis=0)
    cond = jnp.concatenate([c_ctx[None, :], c, jnp.zeros((N_COND - 1 - DEC_BATCH, D_MODEL), F32)], axis=0)
    mod4 = _modulation(cond, w_mod, b_mod).reshape(DEPTH, N_COND, 1, 6 * D_MODEL)
    tt = _bias_table(na_bias)
    ck = cache_k.reshape(DEC_BATCH, DEPTH, PAST_LEN * N_HEADS, HEAD_DIM)
    cvv = cache_v.reshape(DEC_BATCH, DEPTH, PAST_LEN * N_HEADS, HEAD_DIM)
    state = jnp.swapaxes(state_lru, 0, 1)
    h0_ctx = jnp.zeros((BATCH, 2, D_LRU), F32)
    lru_p = {
        "lru_conv_w": lru_conv_w,
        "lru_conv_b": lru_conv_b.reshape(DEPTH, 1, D_LRU),
        "wa_dense": _block_diag_halves(lru_wa),
        "lru_ba": lru_ba,
        "wi_dense": _block_diag_halves(lru_wi),
        "lru_bi": lru_bi,
        "lru_lam": lru_lam,
    }
    ln1 = ln1_g.reshape(DEPTH, 1, D_MODEL)
    ln2 = ln2_g.reshape(DEPTH, 1, D_MODEL)
    cvb = cv_b.reshape(DEPTH, 1, D_CONV)
    cvg = cv_ln_g.reshape(DEPTH, 1, D_CONV)
    cvlb = cv_ln_b.reshape(DEPTH, 1, D_CONV)
    fcb = ffn_conv_b.reshape(DEPTH, 1, 2 * D_FF)
    lat_sb = T_CTX // DEC_SEQ
    w_in = w_in.astype(BF16)
    w_out = w_out.astype(BF16)
    ffn_up = ffn_up.astype(BF16)
    ffn_down = ffn_down.astype(BF16)

    def layer(carry, xs):
        x, ks, vs, hs = carry
        li, h0_lat = xs
        l = jnp.reshape(li, (1,))
        z = _inproj(l, x, ln1, mod4, w_in)
        att, ks, vs = _att_ctx(l, z, ks, vs)
        att = _att_lat(l, z, ck, cvv, tt, att)
        cvo = _convmod(l, z, cv_w, cvb, cvg, cvlb, SEQ, BATCH, 0)
        cvo = _convmod(l, z, cv_w, cvb, cvg, cvlb, DEC_SEQ, DEC_BATCH, lat_sb, prev=cvo)
        rec, h_last = _lru(l, z, h0_ctx, lru_p, SEQ, BATCH, 0)
        rec, _ = _lru(l, z, h0_lat, lru_p, DEC_SEQ, DEC_BATCH, lat_sb, prev=rec)
        x = _outproj(l, att, cvo, rec, w_out, x, mod4)
        x = _ffn(l, x, ln2, mod4, ffn_up, ffn_conv_w, fcb, ffn_down)
        hs = lax.dynamic_update_slice(hs, h_last[:, None], (0, li, 0, 0))
        return (x, ks, vs, hs), None

    init = (x,
            jnp.zeros((BATCH, DEPTH, SEQ * N_HEADS, HEAD_DIM), F32),
            jnp.zeros((BATCH, DEPTH, SEQ * N_HEADS, HEAD_DIM), F32),
            jnp.zeros((BATCH, DEPTH, 2, D_LRU), F32))
    (x, ks, vs, hs), _ = lax.scan(layer, init, (jnp.arange(DEPTH, dtype=jnp.int32), state))

    g = final_g.reshape(1, D_MODEL)
    y_prompt = _final_norm(x, g, 0, T_CTX).reshape(BATCH, SEQ, D_MODEL)
    y_sample = _final_norm(x, g, T_CTX, T_LAT).reshape(DEC_BATCH, DEC_SEQ, D_MODEL)
    new_k = ks.reshape(BATCH, DEPTH, SEQ, N_HEADS, HEAD_DIM)
    new_v = vs.reshape(BATCH, DEPTH, SEQ, N_HEADS, HEAD_DIM)
    return (y_prompt, y_sample, new_k, new_v, hs)
```

```python
import functools

import jax
import jax.numpy as jnp
from jax import lax
from jax.experimental import pallas as pl
from jax.experimental.pallas import tpu as pltpu

F32 = jnp.float32
BF16 = jnp.bfloat16

D_MODEL = 2048
BATCH = 16
SEQ = 256
DEPTH = 4
DEC_BATCH = 2
DEC_SEQ = 1024
PAST_LEN = 512
GRID_W = 64
GRID_R = DEC_SEQ // GRID_W
D_ATT = 1024
N_HEADS = 8
HEAD_DIM = 128
D_CONV = 512
D_LRU = 512
D_MIX = D_ATT + D_CONV + D_LRU
D_IN = 3 * D_ATT + 2 * D_CONV + 2 * D_LRU
WIN_ROWS = 8
WIN_COLS = 16
CONV_WIDTH = 31
LRU_CONV_WIDTH = 4
LRU_BLOCKS = 8
LRU_BW = D_LRU // LRU_BLOCKS
LRU_C = 8.0
D_FF = 5632
EPS = 1e-6
ATT_SCALE = HEAD_DIM ** -0.5
NEG_INF = -1e30

T_CTX = BATCH * SEQ
T_LAT = DEC_BATCH * DEC_SEQ
T_ALL = T_CTX + T_LAT

V7X_VMEM_BYTES = 64 * 1024 * 1024
SUBLANES = 8
LANES = 128

TM = 1024
N_MT = T_ALL // TM
CTX_MT = T_CTX // TM
TM2 = 2 * TM
N_MT2 = T_ALL // TM2
TM_FINAL = 256
TN_IN = 512
TN_OUT = 512
TF = 512
N_FT = D_FF // TF
TN_MOD = 1024
NORM_ROWS = 128
N_COND = 8
BIAS_ROWS = 2 * WIN_ROWS - 1
BIAS_COLS = 2 * WIN_COLS - 1


def _cparams(n_axes, vmem_mib, flags=None):
    return pltpu.CompilerParams(
        dimension_semantics=("arbitrary",) * n_axes,
        vmem_limit_bytes=min(vmem_mib * 1024 * 1024, V7X_VMEM_BYTES - 2 * 1024 * 1024),
        flags=flags,
    )


def _sigmoid(x):
    return 0.5 * jnp.tanh(0.5 * x) + 0.5


def _silu(x):
    return x * _sigmoid(x)


def _mm(a, b):
    return lax.dot_general(a, b, (((1,), (0,)), ((), ())), preferred_element_type=F32)


def _gelu_tanh(x):
    return 0.5 * x * (1.0 + jnp.tanh(0.7978845608028654 * (x + 0.044715 * (x * x * x))))


def _mod_row(i):
    return jnp.maximum(i - (CTX_MT - 1), 0)


def _mod_kernel(c_ref, w_ref, b_ref, o_ref):
    a = _silu(c_ref[...]).astype(BF16)
    o_ref[...] = jnp.dot(a, w_ref[...].astype(BF16), preferred_element_type=F32) + b_ref[...]


def _modulation(cond, w_mod, b_mod):
    n_out = w_mod.shape[-1]
    return pl.pallas_call(
        _mod_kernel,
        out_shape=jax.ShapeDtypeStruct((DEPTH, N_COND, n_out), F32),
        grid=(DEPTH, n_out // TN_MOD),
        in_specs=[
            pl.BlockSpec((N_COND, D_MODEL), lambda l, j: (0, 0)),
            pl.BlockSpec((None, D_MODEL, TN_MOD), lambda l, j: (l, 0, j)),
            pl.BlockSpec((None, 1, TN_MOD), lambda l, j: (l, 0, j)),
        ],
        out_specs=pl.BlockSpec((None, N_COND, TN_MOD), lambda l, j: (l, 0, j)),
        compiler_params=_cparams(2, 40),
        name="modulation",
    )(cond, w_mod, b_mod.reshape(DEPTH, 1, n_out))


def _bias_kernel(rb_ref, o_ref, td_scr):
    base = (pl.program_id(0) * N_HEADS + pl.program_id(1)) * (BIAS_ROWS * BIAS_COLS)
    qi = lax.broadcasted_iota(jnp.int32, (GRID_W, LANES), 0)
    li = lax.broadcasted_iota(jnp.int32, (GRID_W, LANES), 1)
    diff = jnp.clip((li & (GRID_W - 1)) - qi, -(WIN_COLS - 1), WIN_COLS - 1) + (WIN_COLS - 1)

    def body(i, carry):
        acc = jnp.zeros((GRID_W, LANES), F32)
        for d in range(BIAS_COLS):
            acc = jnp.where(diff == d, rb_ref[base + i * BIAS_COLS + d], acc)
        td_scr[i] = acc
        return carry

    lax.fori_loop(0, BIAS_ROWS, body, 0)
    td_scr[BIAS_ROWS] = jnp.zeros((GRID_W, LANES), F32)
    left = li < GRID_W
    for i in range(BIAS_ROWS):
        o_ref[i] = jnp.where(left, td_scr[i], td_scr[i + 1])


def _bias_table(na_bias):
    return pl.pallas_call(
        _bias_kernel,
        out_shape=jax.ShapeDtypeStruct((DEPTH, N_HEADS, BIAS_ROWS, GRID_W, LANES), F32),
        grid=(DEPTH, N_HEADS),
        in_specs=[pl.BlockSpec(memory_space=pltpu.SMEM)],
        out_specs=pl.BlockSpec((None, None, BIAS_ROWS, GRID_W, LANES), lambda l, h: (l, h, 0, 0, 0)),
        scratch_shapes=[pltpu.VMEM((BIAS_ROWS + 1, GRID_W, LANES), F32)],
        compiler_params=_cparams(2, 16),
        name="bias_table",
    )(na_bias.reshape(-1))


def _norm_mod(x_ref, g_ref, sc_ref, sh_ref, h_scr, row0=0):
    g = g_ref[...]
    scale = 1.0 + sc_ref[...]
    shift = sh_ref[...]

    def body(c, carry):
        r0 = pl.multiple_of(row0 + c * NORM_ROWS, NORM_ROWS)
        x = x_ref[pl.ds(r0, NORM_ROWS), :]
        ms = jnp.mean(x * x, axis=-1, keepdims=True)
        y = x * lax.rsqrt(ms + EPS) * g
        h_scr[pl.ds(r0, NORM_ROWS), :] = (y * scale + shift).astype(BF16)
        return carry

    lax.fori_loop(0, TM // NORM_ROWS, body, 0)


def _mod_spec(chunk):
    return pl.BlockSpec((None, None, 1, D_MODEL), lambda i, j, l: (l[0], _mod_row(i), 0, chunk))


def _mod_spec2(chunk, half):
    return pl.BlockSpec((None, None, 1, D_MODEL), lambda i, j, l: (l[0], _mod_row(2 * i + half), 0, chunk))


def _inproj_kernel(l_ref, x_ref, g_ref, sh0_ref, sc0_ref, sh1_ref, sc1_ref, w_ref, o_ref, h_scr):
    @pl.when(pl.program_id(1) == 0)
    def _():
        _norm_mod(x_ref, g_ref, sc0_ref, sh0_ref, h_scr, 0)
        _norm_mod(x_ref, g_ref, sc1_ref, sh1_ref, h_scr, TM)

    o_ref[...] = _mm(h_scr[...], w_ref[...])


def _inproj(l, x, ln_g, mod4, w_in):
    return pl.pallas_call(
        _inproj_kernel,
        out_shape=jax.ShapeDtypeStruct((T_ALL, D_IN), F32),
        grid_spec=pltpu.PrefetchScalarGridSpec(
            num_scalar_prefetch=1,
            grid=(N_MT2, D_IN // TN_IN),
            in_specs=[
                pl.BlockSpec((TM2, D_MODEL), lambda i, j, l: (i, 0)),
                pl.BlockSpec((None, 1, D_MODEL), lambda i, j, l: (l[0], 0, 0)),
                _mod_spec2(0, 0),
                _mod_spec2(1, 0),
                _mod_spec2(0, 1),
                _mod_spec2(1, 1),
                pl.BlockSpec((None, D_MODEL, TN_IN), lambda i, j, l: (l[0], 0, j)),
            ],
            out_specs=pl.BlockSpec((TM2, TN_IN), lambda i, j, l: (i, j)),
            scratch_shapes=[pltpu.VMEM((TM2, D_MODEL), BF16)],
        ),
        compiler_params=_cparams(2, 62),
        name="inproj",
    )(l, x, ln_g, mod4, mod4, mod4, mod4, w_in)


def _dot_nt(a, b):
    return lax.dot_general(a, b, (((1,), (1,)), ((), ())), preferred_element_type=F32)


def _att_ctx_kernel(l_ref, q_ref, k_ref, v_ref, ks_in_ref, vs_in_ref, o_ref, nk_ref, nv_ref):
    del l_ref, ks_in_ref, vs_in_ref
    for h in range(N_HEADS):
        sl = slice(h * HEAD_DIM, (h + 1) * HEAD_DIM)
        kf = k_ref[:, sl]
        vf = v_ref[:, sl]
        nk_ref[pl.ds(h, SEQ, stride=N_HEADS), :] = kf
        nv_ref[pl.ds(h, SEQ, stride=N_HEADS), :] = vf
        q = q_ref[:, sl].astype(BF16)
        k = kf.astype(BF16)
        v = vf.astype(BF16)
        s = _dot_nt(q, k) * ATT_SCALE
        p = jnp.exp(s - jnp.max(s, axis=-1, keepdims=True))
        denom = jnp.sum(p, axis=-1, keepdims=True)
        o = jnp.dot(p.astype(BF16), v, preferred_element_type=F32)
        o_ref[:, sl] = (o / denom).astype(o_ref.dtype)


def _att_ctx(l, z, ks, vs):
    kv_shape = jax.ShapeDtypeStruct((BATCH, DEPTH, SEQ * N_HEADS, HEAD_DIM), F32)
    kv_spec = pl.BlockSpec((None, None, SEQ * N_HEADS, HEAD_DIM), lambda b, l: (b, l[0], 0, 0))
    return pl.pallas_call(
        _att_ctx_kernel,
        out_shape=(jax.ShapeDtypeStruct((T_ALL, D_ATT), BF16), kv_shape, kv_shape),
        grid_spec=pltpu.PrefetchScalarGridSpec(
            num_scalar_prefetch=1,
            grid=(BATCH,),
            in_specs=[
                pl.BlockSpec((SEQ, D_ATT), lambda b, l: (b, 0)),
                pl.BlockSpec((SEQ, D_ATT), lambda b, l: (b, 1)),
                pl.BlockSpec((SEQ, D_ATT), lambda b, l: (b, 2)),
                pl.BlockSpec(memory_space=pl.ANY),
                pl.BlockSpec(memory_space=pl.ANY),
            ],
            out_specs=(pl.BlockSpec((SEQ, D_ATT), lambda b, l: (b, 0)), kv_spec, kv_spec),
        ),
        input_output_aliases={4: 1, 5: 2},
        compiler_params=_cparams(1, 32),
        name="att_ctx",
    )(l, z, z, z, ks, vs)


Q_ROWS = 4
_KEY_ROWS = ((0, 8), (0, 12), (4, 16), (8, 16))


def _win_start(rq):
    return min(max(rq - WIN_ROWS // 2, 0), GRID_R - WIN_ROWS)


def _att_lat_kernel(l_ref, q_ref, k_ref, v_ref, ck_ref, cv_ref, tt_ref, att_in_ref, o_ref):
    del l_ref, att_in_ref
    qi = lax.broadcasted_iota(jnp.int32, (GRID_W, LANES), 0)
    li = lax.broadcasted_iota(jnp.int32, (GRID_W, LANES), 1)
    kc = li & (GRID_W - 1)
    cs = jnp.clip(qi - WIN_COLS // 2, 0, GRID_W - WIN_COLS)
    in_cols = (kc >= cs) & (kc < cs + WIN_COLS)
    left = li < GRID_W
    mask_both = in_cols
    mask_left = in_cols & left
    mask_right = in_cols & jnp.logical_not(left)
    neg = jnp.full((GRID_W, LANES), NEG_INF, F32)

    kb = k_ref[...].astype(BF16)
    vb = v_ref[...].astype(BF16)
    head = pl.program_id(1)
    ckb = ck_ref[pl.ds(head, PAST_LEN, stride=N_HEADS), :].astype(BF16)
    cvb = cv_ref[pl.ds(head, PAST_LEN, stride=N_HEADS), :].astype(BF16)
    tq = Q_ROWS * GRID_W
    for t in range(GRID_R // Q_ROWS):
        lo, hi = _KEY_ROWS[t]
        q = q_ref[t * tq:(t + 1) * tq, :].astype(BF16)
        s_w = _dot_nt(q, kb[lo * GRID_W:hi * GRID_W, :])
        s_c = _dot_nt(q, ckb) * ATT_SCALE
        rows = []
        for rl in range(Q_ROWS):
            rq = t * Q_ROWS + rl
            start = _win_start(rq)
            blks = []
            for m in range(lo // 2, hi // 2):
                v0 = start <= 2 * m < start + WIN_ROWS
                v1 = start <= 2 * m + 1 < start + WIN_ROWS
                if not (v0 or v1):
                    blks.append(neg)
                    continue
                i = 2 * m - rq + WIN_ROWS - 1
                assert 0 <= i < BIAS_ROWS
                mask = mask_both if (v0 and v1) else (mask_left if v0 else mask_right)
                c0 = (2 * m - lo) * GRID_W
                sb = s_w[rl * GRID_W:(rl + 1) * GRID_W, c0:c0 + LANES]
                blks.append(jnp.where(mask, sb * ATT_SCALE + tt_ref[i], NEG_INF))
            rows.append(jnp.concatenate(blks, axis=1))
        sw = jnp.concatenate(rows, axis=0)
        mx = jnp.maximum(jnp.max(sw, axis=-1, keepdims=True), jnp.max(s_c, axis=-1, keepdims=True))
        pw = jnp.exp(sw - mx)
        pc = jnp.exp(s_c - mx)
        denom = jnp.sum(pw, axis=-1, keepdims=True) + jnp.sum(pc, axis=-1, keepdims=True)
        o = jnp.dot(pw.astype(BF16), vb[lo * GRID_W:hi * GRID_W, :], preferred_element_type=F32)
        o = o + jnp.dot(pc.astype(BF16), cvb, preferred_element_type=F32)
        o_ref[t * tq:(t + 1) * tq, :] = (o / denom).astype(o_ref.dtype)


def _att_lat(l, z, cache_k, cache_v, tt, att):
    rb0 = T_CTX // DEC_SEQ
    hb = D_ATT // HEAD_DIM
    return pl.pallas_call(
        _att_lat_kernel,
        out_shape=jax.ShapeDtypeStruct((T_ALL, D_ATT), BF16),
        grid_spec=pltpu.PrefetchScalarGridSpec(
            num_scalar_prefetch=1,
            grid=(DEC_BATCH, N_HEADS),
            in_specs=[
                pl.BlockSpec((DEC_SEQ, HEAD_DIM), lambda b, h, l: (rb0 + b, h)),
                pl.BlockSpec((DEC_SEQ, HEAD_DIM), lambda b, h, l: (rb0 + b, hb + h)),
                pl.BlockSpec((DEC_SEQ, HEAD_DIM), lambda b, h, l: (rb0 + b, 2 * hb + h)),
                pl.BlockSpec((None, None, PAST_LEN * N_HEADS, HEAD_DIM), lambda b, h, l: (b, l[0], 0, 0)),
                pl.BlockSpec((None, None, PAST_LEN * N_HEADS, HEAD_DIM), lambda b, h, l: (b, l[0], 0, 0)),
                pl.BlockSpec((None, None, BIAS_ROWS, GRID_W, LANES), lambda b, h, l: (l[0], h, 0, 0, 0)),
                pl.BlockSpec(memory_space=pl.ANY),
            ],
            out_specs=pl.BlockSpec((DEC_SEQ, HEAD_DIM), lambda b, h, l: (rb0 + b, h)),
        ),
        input_output_aliases={7: 0},
        compiler_params=_cparams(2, 32),
        name="att_lat",
    )(l, z, z, z, cache_k, cache_v, tt, att)


CV_PAD = 16
CV_ROWS = 64


def _convmod_kernel(l_ref, a_ref, g_ref, w_ref, b_ref, lng_ref, lnb_ref, *rest):
    del l_ref
    o_ref, upad, ush = rest[-3:]
    s_len = a_ref.shape[0]
    n_sh = s_len + CV_PAD + SUBLANES
    upad[0:CV_PAD, :] = jnp.zeros((CV_PAD, D_CONV), F32)
    upad[CV_PAD + s_len:, :] = jnp.zeros((CV_PAD, D_CONV), F32)
    upad[CV_PAD:CV_PAD + s_len, :] = a_ref[...] * _sigmoid(g_ref[...])
    for j in range(SUBLANES):
        ush[j, 0:n_sh, :] = upad[j:j + n_sh, :]
    bias = b_ref[...]
    ln_g = lng_ref[...]
    ln_b = lnb_ref[...]

    def chunk(c, carry):
        r0 = pl.multiple_of(c * CV_ROWS, CV_ROWS)
        acc = jnp.broadcast_to(bias, (CV_ROWS, D_CONV))
        for k in range(CONV_WIDTH):
            m, j = divmod(k + CV_PAD - CONV_WIDTH // 2, SUBLANES)
            acc = acc + w_ref[k:k + 1, :] * ush[j, pl.ds(r0 + SUBLANES * m, CV_ROWS), :]
        mu = jnp.mean(acc, axis=-1, keepdims=True)
        xc = acc - mu
        var = jnp.mean(xc * xc, axis=-1, keepdims=True)
        y = xc * lax.rsqrt(var + EPS) * ln_g + ln_b
        o_ref[pl.ds(r0, CV_ROWS), :] = _silu(y).astype(o_ref.dtype)
        return carry

    lax.fori_loop(0, s_len // CV_ROWS, chunk, 0)


def _convmod(l, z, cv_w, cv_b, ln_g, ln_b, s_len, n_seq, row_block0, prev=None):
    cb = (3 * D_ATT) // D_CONV
    in_specs = [
        pl.BlockSpec((s_len, D_CONV), lambda b, l: (row_block0 + b, cb)),
        pl.BlockSpec((s_len, D_CONV), lambda b, l: (row_block0 + b, cb + 1)),
        pl.BlockSpec((None, CONV_WIDTH, D_CONV), lambda b, l: (l[0], 0, 0)),
        pl.BlockSpec((None, 1, D_CONV), lambda b, l: (l[0], 0, 0)),
        pl.BlockSpec((None, 1, D_CONV), lambda b, l: (l[0], 0, 0)),
        pl.BlockSpec((None, 1, D_CONV), lambda b, l: (l[0], 0, 0)),
    ]
    args = [l, z, z, cv_w, cv_b, ln_g, ln_b]
    aliases = {}
    if prev is not None:
        in_specs.append(pl.BlockSpec(memory_space=pl.ANY))
        args.append(prev)
        aliases = {len(args) - 1: 0}
    return pl.pallas_call(
        _convmod_kernel,
        out_shape=jax.ShapeDtypeStruct((T_ALL, D_CONV), BF16),
        grid_spec=pltpu.PrefetchScalarGridSpec(
            num_scalar_prefetch=1,
            grid=(n_seq,),
            in_specs=in_specs,
            out_specs=pl.BlockSpec((s_len, D_CONV), lambda b, l: (row_block0 + b, 0)),
            scratch_shapes=[
                pltpu.VMEM((s_len + 2 * CV_PAD, D_CONV), F32),
                pltpu.VMEM((SUBLANES, s_len + CV_PAD + SUBLANES, D_CONV), F32),
            ],
        ),
        input_output_aliases=aliases,
        compiler_params=_cparams(1, 48),
        name=f"convmod_{s_len}",
    )(*args)


LRU_PAD = 8
LRU_HALF = D_LRU // 2
LRU_CHUNK = 256


def _softplus(x):
    return jnp.maximum(x, 0.0) + jnp.log1p(jnp.exp(-jnp.abs(x)))


def _lru_kernel(l_ref, x_ref, g_ref, h0_ref, cw_ref, cb_ref, wa_ref, ba_ref, wi_ref, bi_ref, lam_ref, *rest):
    del l_ref
    y_ref, hl_ref, xpad, a_scr, b_scr, h_scr = rest[-6:]
    s_len = x_ref.shape[0]
    n_blk = s_len // SUBLANES
    n_lt = D_LRU // LANES

    xpad[0:LRU_PAD, :] = jnp.zeros((LRU_PAD, D_LRU), F32)
    xpad[LRU_PAD + s_len:, :] = jnp.zeros((LRU_PAD, D_LRU), F32)
    xpad[LRU_PAD:LRU_PAD + s_len, :] = x_ref[...]
    cb = cb_ref[...]
    decay = [-LRU_C * _softplus(-lam_ref[d:d + 1, :]) for d in range(2)]

    for c in range(s_len // LRU_CHUNK):
        r0 = c * LRU_CHUNK
        xc = jnp.broadcast_to(cb, (LRU_CHUNK, D_LRU))
        for k in range(LRU_CONV_WIDTH):
            off = LRU_PAD + r0 + k - LRU_CONV_WIDTH // 2
            xc = xc + cw_ref[k:k + 1, :] * xpad[off:off + LRU_CHUNK, :]
        xb = xc.astype(BF16)
        for d in range(2):
            ga = jnp.concatenate(
                [jnp.dot(xb[:, hh * LRU_HALF:(hh + 1) * LRU_HALF], wa_ref[d, hh], preferred_element_type=F32)
                 for hh in range(2)], axis=1) + ba_ref[d:d + 1, :]
            gi = jnp.concatenate(
                [jnp.dot(xb[:, hh * LRU_HALF:(hh + 1) * LRU_HALF], wi_ref[d, hh], preferred_element_type=F32)
                 for hh in range(2)], axis=1) + bi_ref[d:d + 1, :]
            log_a = _sigmoid(ga) * decay[d]
            th = jnp.tanh(log_a)
            gain = jnp.sqrt(-2.0 * th / (1.0 - th))
            a_scr[d, r0:r0 + LRU_CHUNK, :] = jnp.exp(log_a)
            b_scr[d, r0:r0 + LRU_CHUNK, :] = gain * (_sigmoid(gi) * xc)

    row = lax.broadcasted_iota(jnp.int32, (SUBLANES, LANES), 0)

    def scan_block(it, carry):
        new = []
        for d in range(2):
            blk = it if d == 0 else n_blk - 1 - it
            r0 = pl.multiple_of(blk * SUBLANES, SUBLANES)
            for c in range(n_lt):
                ls = slice(c * LANES, (c + 1) * LANES)
                a = a_scr[d, pl.ds(r0, SUBLANES), ls]
                b = b_scr[d, pl.ds(r0, SUBLANES), ls]
                for s in (1, 2, 4):
                    shift = s if d == 0 else SUBLANES - s
                    valid = (row >= s) if d == 0 else (row < SUBLANES - s)
                    a_sh = jnp.where(valid, pltpu.roll(a, shift, 0), 1.0)
                    b_sh = jnp.where(valid, pltpu.roll(b, shift, 0), 0.0)
                    b = a * b_sh + b
                    a = a * a_sh
                h = a * carry[d * n_lt + c] + b
                h_scr[d, pl.ds(r0, SUBLANES), ls] = h
                last = h[SUBLANES - 1:SUBLANES, :] if d == 0 else h[0:1, :]
                new.append(jnp.broadcast_to(last, (SUBLANES, LANES)))
        return tuple(new)

    init = tuple(
        jnp.broadcast_to(h0_ref[d:d + 1, c * LANES:(c + 1) * LANES], (SUBLANES, LANES))
        for d in range(2) for c in range(n_lt))
    final = lax.fori_loop(0, n_blk, scan_block, init)
    for d in range(2):
        for c in range(n_lt):
            hl_ref[d:d + 1, c * LANES:(c + 1) * LANES] = final[d * n_lt + c][0:1, :]

    for c in range(s_len // LRU_CHUNK):
        rs = slice(c * LRU_CHUNK, (c + 1) * LRU_CHUNK)
        y = (h_scr[0, rs, :] + h_scr[1, rs, :]) * _gelu_tanh(g_ref[rs, :])
        y_ref[rs, :] = y.astype(y_ref.dtype)


def _lru(l, z, h0, p, s_len, n_seq, row_block0, prev=None):
    cb = (3 * D_ATT + 2 * D_CONV) // D_LRU
    in_specs = [
        pl.BlockSpec((s_len, D_LRU), lambda b, l: (row_block0 + b, cb)),
        pl.BlockSpec((s_len, D_LRU), lambda b, l: (row_block0 + b, cb + 1)),
        pl.BlockSpec((None, 2, D_LRU), lambda b, l: (b, 0, 0)),
        pl.BlockSpec((None, LRU_CONV_WIDTH, D_LRU), lambda b, l: (l[0], 0, 0)),
        pl.BlockSpec((None, 1, D_LRU), lambda b, l: (l[0], 0, 0)),
        pl.BlockSpec((None, 2, 2, LRU_HALF, LRU_HALF), lambda b, l: (l[0], 0, 0, 0, 0)),
        pl.BlockSpec((None, 2, D_LRU), lambda b, l: (l[0], 0, 0)),
        pl.BlockSpec((None, 2, 2, LRU_HALF, LRU_HALF), lambda b, l: (l[0], 0, 0, 0, 0)),
        pl.BlockSpec((None, 2, D_LRU), lambda b, l: (l[0], 0, 0)),
        pl.BlockSpec((None, 2, D_LRU), lambda b, l: (l[0], 0, 0)),
    ]
    args = [l, z, z, h0, p["lru_conv_w"], p["lru_conv_b"], p["wa_dense"], p["lru_ba"], p["wi_dense"],
            p["lru_bi"], p["lru_lam"]]
    aliases = {}
    if prev is not None:
        in_specs.append(pl.BlockSpec(memory_space=pl.ANY))
        args.append(prev)
        aliases = {len(args) - 1: 0}
    return pl.pallas_call(
        _lru_kernel,
        out_shape=(jax.ShapeDtypeStruct((T_ALL, D_LRU), BF16),
                   jax.ShapeDtypeStruct((n_seq, 2, D_LRU), F32)),
        grid_spec=pltpu.PrefetchScalarGridSpec(
            num_scalar_prefetch=1,
            grid=(n_seq,),
            in_specs=in_specs,
            out_specs=(pl.BlockSpec((s_len, D_LRU), lambda b, l: (row_block0 + b, 0)),
                       pl.BlockSpec((None, 2, D_LRU), lambda b, l: (b, 0, 0))),
            scratch_shapes=[
                pltpu.VMEM((s_len + 2 * LRU_PAD, D_LRU), F32),
                pltpu.VMEM((2, s_len, D_LRU), F32),
                pltpu.VMEM((2, s_len, D_LRU), F32),
                pltpu.VMEM((2, s_len, D_LRU), F32),
            ],
        ),
        input_output_aliases=aliases,
        compiler_params=_cparams(1, 48),
        name=f"rglru_{s_len}",
    )(*args)


def _outproj_kernel(l_ref, att_ref, cv_ref, rec_ref, w_ref, x_ref, gate0_ref, gate1_ref, o_ref):
    del l_ref
    y = jnp.dot(att_ref[...], w_ref[0:D_ATT, :], preferred_element_type=F32)
    y = y + jnp.dot(cv_ref[...], w_ref[D_ATT:D_ATT + D_CONV, :], preferred_element_type=F32)
    y = y + jnp.dot(rec_ref[...], w_ref[D_ATT + D_CONV:D_MIX, :], preferred_element_type=F32)
    o_ref[0:TM, :] = x_ref[0:TM, :] + gate0_ref[...] * y[0:TM, :]
    o_ref[TM:TM2, :] = x_ref[TM:TM2, :] + gate1_ref[...] * y[TM:TM2, :]


def _outproj(l, att, cv, rec, w_out, x, mod4):
    def gate_spec(half):
        return pl.BlockSpec((None, None, 1, TN_OUT),
                            lambda i, j, l: (l[0], _mod_row(2 * i + half), 0, 2 * (D_MODEL // TN_OUT) + j))

    return pl.pallas_call(
        _outproj_kernel,
        out_shape=jax.ShapeDtypeStruct((T_ALL, D_MODEL), F32),
        grid_spec=pltpu.PrefetchScalarGridSpec(
            num_scalar_prefetch=1,
            grid=(N_MT2, D_MODEL // TN_OUT),
            in_specs=[
                pl.BlockSpec((TM2, D_ATT), lambda i, j, l: (i, 0)),
                pl.BlockSpec((TM2, D_CONV), lambda i, j, l: (i, 0)),
                pl.BlockSpec((TM2, D_LRU), lambda i, j, l: (i, 0)),
                pl.BlockSpec((None, D_MIX, TN_OUT), lambda i, j, l: (l[0], 0, j)),
                pl.BlockSpec((TM2, TN_OUT), lambda i, j, l: (i, j)),
                gate_spec(0),
                gate_spec(1),
            ],
            out_specs=pl.BlockSpec((TM2, TN_OUT), lambda i, j, l: (i, j)),
        ),
        compiler_params=_cparams(2, 48),
        name="outproj",
    )(l, att, cv, rec, w_out, x, mod4, mod4)


TN_DOWN = 512
FFN_SPLIT = 4
FFN_HALO = 16
FFN_AHEAD = 3


def _ffn_kernel(l_ref, x_ref, g_ref, sh_ref, sc_ref, gate_ref, wug_ref, wuv_ref, cwg_ref, cwv_ref,
                cbg_ref, cbv_ref, wd_ref, o_ref, h_scr, mprev_scr, mnext_scr):
    del l_ref
    i = pl.program_id(0)
    j = pl.program_id(1)

    @pl.when(j == 0)
    def _():
        _norm_mod(x_ref, g_ref, sc_ref, sh_ref, h_scr)
        o_ref[...] = jnp.zeros_like(o_ref)
        seq_len = jnp.where(i < CTX_MT, SEQ, DEC_SEQ)
        pos = lax.broadcasted_iota(jnp.int32, (TM, LANES), 0) & (seq_len - 1)
        mprev_scr[...] = jnp.where(pos == 0, 0.0, 1.0)
        mnext_scr[...] = jnp.where(pos == seq_len - 1, 0.0, 1.0)

    rows = TM // FFN_SPLIT
    def bounds(s):
        return max(s * rows - FFN_HALO, 0), min((s + 1) * rows + FFN_HALO, TM)

    def up(s):
        lo, hi = bounds(s)
        h = h_scr[lo:hi, :]
        return (_mm(h, wug_ref[...]), _mm(h, wuv_ref[...]))

    def conv(u, s, cw_ref, cb_ref):
        lo, hi = bounds(s)
        off = s * rows - lo
        prev = pltpu.roll(u, 1, 0) * jnp.tile(mprev_scr[lo:hi, :], (1, TF // LANES))
        nxt = pltpu.roll(u, hi - lo - 1, 0) * jnp.tile(mnext_scr[lo:hi, :], (1, TF // LANES))
        c = cw_ref[0:1, :] * prev + cw_ref[1:2, :] * u + cw_ref[2:3, :] * nxt + cb_ref[...]
        return c[off:off + rows, :]

    pending = [up(s) for s in range(min(FFN_AHEAD, FFN_SPLIT))]
    for s in range(FFN_SPLIT):
        if s + FFN_AHEAD < FFN_SPLIT:
            pending.append(up(s + FFN_AHEAD))
        u_cur = pending.pop(0)
        ug = conv(u_cur[0], s, cwg_ref, cbg_ref)
        uv = conv(u_cur[1], s, cwv_ref, cbv_ref)
        act = (_silu(ug) * uv).astype(BF16)
        rs = slice(s * rows, (s + 1) * rows)
        for n in range(D_MODEL // TN_DOWN):
            ns = slice(n * TN_DOWN, (n + 1) * TN_DOWN)
            o_ref[rs, ns] = _mm(act, wd_ref[:, ns]) + o_ref[rs, ns]

    @pl.when(j == N_FT - 1)
    def _():
        o_ref[...] = x_ref[...] + gate_ref[...] * o_ref[...]


def _ffn(l, x, ln_g, mod4, ffn_up, conv_w, conv_b, ffn_down):
    return pl.pallas_call(
        _ffn_kernel,
        out_shape=jax.ShapeDtypeStruct((T_ALL, D_MODEL), F32),
        grid_spec=pltpu.PrefetchScalarGridSpec(
            num_scalar_prefetch=1,
            grid=(N_MT, N_FT),
            in_specs=[
                pl.BlockSpec((TM, D_MODEL), lambda i, j, l: (i, 0), pipeline_mode=pl.Buffered(1)),
                pl.BlockSpec((None, 1, D_MODEL), lambda i, j, l: (l[0], 0, 0)),
                _mod_spec(3),
                _mod_spec(4),
                _mod_spec(5),
                pl.BlockSpec((None, D_MODEL, TF), lambda i, j, l: (l[0], 0, j)),
                pl.BlockSpec((None, D_MODEL, TF), lambda i, j, l: (l[0], 0, N_FT + j)),
                pl.BlockSpec((None, 3, TF), lambda i, j, l: (l[0], 0, j)),
                pl.BlockSpec((None, 3, TF), lambda i, j, l: (l[0], 0, N_FT + j)),
                pl.BlockSpec((None, 1, TF), lambda i, j, l: (l[0], 0, j)),
                pl.BlockSpec((None, 1, TF), lambda i, j, l: (l[0], 0, N_FT + j)),
                pl.BlockSpec((None, TF, D_MODEL), lambda i, j, l: (l[0], j, 0)),
            ],
            out_specs=pl.BlockSpec((TM, D_MODEL), lambda i, j, l: (i, 0)),
            scratch_shapes=[
                pltpu.VMEM((TM, D_MODEL), BF16),
                pltpu.VMEM((TM, LANES), F32),
                pltpu.VMEM((TM, LANES), F32),
            ],
        ),
        compiler_params=_cparams(2, 62),
        name="convffn",
    )(l, x, ln_g, mod4, mod4, mod4, ffn_up, ffn_up, conv_w, conv_w, conv_b, conv_b, ffn_down)


def _final_kernel(x_ref, g_ref, o_ref):
    g = g_ref[...]
    for c in range(TM_FINAL // NORM_ROWS):
        rs = slice(c * NORM_ROWS, (c + 1) * NORM_ROWS)
        x = x_ref[rs, :]
        ms = jnp.mean(x * x, axis=-1, keepdims=True)
        o_ref[rs, :] = x * lax.rsqrt(ms + EPS) * g


def _final_norm(x, g, row0, n_rows):
    tile0 = row0 // TM_FINAL
    return pl.pallas_call(
        _final_kernel,
        out_shape=jax.ShapeDtypeStruct((n_rows, D_MODEL), F32),
        grid=(n_rows // TM_FINAL,),
        in_specs=[
            pl.BlockSpec((TM_FINAL, D_MODEL), lambda i: (tile0 + i, 0)),
            pl.BlockSpec((1, D_MODEL), lambda i: (0, 0)),
        ],
        out_specs=pl.BlockSpec((TM_FINAL, D_MODEL), lambda i: (i, 0)),
        compiler_params=_cparams(1, 32),
        name="final_norm",
    )(x, g)


def _block_diag_halves(w):
    nb = LRU_BLOCKS // 2
    w6 = w.reshape(DEPTH, 2, 2, nb, LRU_BW, LRU_BW)
    eye = jnp.eye(nb, dtype=w.dtype)
    dense = jnp.einsum("ldhjab,jk->ldhjakb", w6, eye)
    dense = dense.reshape(DEPTH, 2, 2, LRU_HALF, LRU_HALF).astype(BF16)
    return lax.optimization_barrier(dense)


def kernel(x_prompt, x_sample, cache_k, cache_v, state_lru, c, c_ctx, ln1_g, w_mod, b_mod, w_in, na_bias,
           cv_w, cv_b, cv_ln_g, cv_ln_b, lru_conv_w, lru_conv_b, lru_wa, lru_ba, lru_wi, lru_bi, lru_lam,
           w_out, ln2_g, ffn_up, ffn_conv_w, ffn_conv_b, ffn_down, final_g):
    x = jnp.concatenate([x_prompt.reshape(T_CTX, D_MODEL), x_sample.reshape(T_LAT, D_MODEL)], axis=0)
    cond = jnp.concatenate([c_ctx[None, :], c, jnp.zeros((N_COND - 1 - DEC_BATCH, D_MODEL), F32)], axis=0)
    mod4 = _modulation(cond, w_mod, b_mod).reshape(DEPTH, N_COND, 1, 6 * D_MODEL)
    tt = _bias_table(na_bias)
    ck = cache_k.reshape(DEC_BATCH, DEPTH, PAST_LEN * N_HEADS, HEAD_DIM)
    cvv = cache_v.reshape(DEC_BATCH, DEPTH, PAST_LEN * N_HEADS, HEAD_DIM)
    state = jnp.swapaxes(state_lru, 0, 1)
    h0_ctx = jnp.zeros((BATCH, 2, D_LRU), F32)
    lru_p = {
        "lru_conv_w": lru_conv_w,
        "lru_conv_b": lru_conv_b.reshape(DEPTH, 1, D_LRU),
        "wa_dense": _block_diag_halves(lru_wa),
        "lru_ba": lru_ba,
        "wi_dense": _block_diag_halves(lru_wi),
        "lru_bi": lru_bi,
        "lru_lam": lru_lam,
    }
    ln1 = ln1_g.reshape(DEPTH, 1, D_MODEL)
    ln2 = ln2_g.reshape(DEPTH, 1, D_MODEL)
    cvb = cv_b.reshape(DEPTH, 1, D_CONV)
    cvg = cv_ln_g.reshape(DEPTH, 1, D_CONV)
    cvlb = cv_ln_b.reshape(DEPTH, 1, D_CONV)
    fcb = ffn_conv_b.reshape(DEPTH, 1, 2 * D_FF)
    lat_sb = T_CTX // DEC_SEQ
    w_out = w_out.astype(BF16)

    def layer(carry, xs):
        x, ks, vs, hs = carry
        li, h0_lat = xs
        l = jnp.reshape(li, (1,))
        z = _inproj(l, x, ln1, mod4, w_in)
        att, ks, vs = _att_ctx(l, z, ks, vs)
        att = _att_lat(l, z, ck, cvv, tt, att)
        cvo = _convmod(l, z, cv_w, cvb, cvg, cvlb, SEQ, BATCH, 0)
        cvo = _convmod(l, z, cv_w, cvb, cvg, cvlb, DEC_SEQ, DEC_BATCH, lat_sb, prev=cvo)
        rec, h_last = _lru(l, z, h0_ctx, lru_p, SEQ, BATCH, 0)
        rec, _ = _lru(l, z, h0_lat, lru_p, DEC_SEQ, DEC_BATCH, lat_sb, prev=rec)
        x = _outproj(l, att, cvo, rec, w_out, x, mod4)
        x = _ffn(l, x, ln2, mod4, ffn_up, ffn_conv_w, fcb, ffn_down)
        hs = lax.dynamic_update_slice(hs, h_last[:, None], (0, li, 0, 0))
        return (x, ks, vs, hs), None

    init = (x,
            jnp.zeros((BATCH, DEPTH, SEQ * N_HEADS, HEAD_DIM), F32),
            jnp.zeros((BATCH, DEPTH, SEQ * N_HEADS, HEAD_DIM), F32),
            jnp.zeros((BATCH, DEPTH, 2, D_LRU), F32))
    (x, ks, vs, hs), _ = lax.scan(layer, init, (jnp.arange(DEPTH, dtype=jnp.int32), state))

    g = final_g.reshape(1, D_MODEL)
    y_prompt = _final_norm(x, g, 0, T_CTX).reshape(BATCH, SEQ, D_MODEL)
    y_sample = _final_norm(x, g, T_CTX, T_LAT).reshape(DEC_BATCH, DEC_SEQ, D_MODEL)
    new_k = ks.reshape(BATCH, DEPTH, SEQ, N_HEADS, HEAD_DIM)
    new_v = vs.reshape(BATCH, DEPTH, SEQ, N_HEADS, HEAD_DIM)
    return (y_prompt, y_sample, new_k, new_v, hs)
```

```python
import functools

import jax
import jax.numpy as jnp
from jax import lax
from jax.experimental import pallas as pl
from jax.experimental.pallas import tpu as pltpu

F32 = jnp.float32
BF16 = jnp.bfloat16

D_MODEL = 2048
BATCH = 16
SEQ = 256
DEPTH = 4
DEC_BATCH = 2
DEC_SEQ = 1024
PAST_LEN = 512
GRID_W = 64
GRID_R = DEC_SEQ // GRID_W
D_ATT = 1024
N_HEADS = 8
HEAD_DIM = 128
D_CONV = 512
D_LRU = 512
D_MIX = D_ATT + D_CONV + D_LRU
D_IN = 3 * D_ATT + 2 * D_CONV + 2 * D_LRU
WIN_ROWS = 8
WIN_COLS = 16
CONV_WIDTH = 31
LRU_CONV_WIDTH = 4
LRU_BLOCKS = 8
LRU_BW = D_LRU // LRU_BLOCKS
LRU_C = 8.0
D_FF = 5632
EPS = 1e-6
ATT_SCALE = HEAD_DIM ** -0.5
NEG_INF = -1e30

T_CTX = BATCH * SEQ
T_LAT = DEC_BATCH * DEC_SEQ
T_ALL = T_CTX + T_LAT

V7X_VMEM_BYTES = 64 * 1024 * 1024
SUBLANES = 8
LANES = 128

TM = 1024
N_MT = T_ALL // TM
CTX_MT = T_CTX // TM
TM2 = 2 * TM
N_MT2 = T_ALL // TM2
TM_FINAL = 256
TN_IN = 512
TF = 512
N_FT = D_FF // TF
TN_MOD = 1024
NORM_ROWS = 128
N_COND = 8
BIAS_ROWS = 2 * WIN_ROWS - 1
BIAS_COLS = 2 * WIN_COLS - 1


def _cparams(n_axes, vmem_mib, flags=None):
    return pltpu.CompilerParams(
        dimension_semantics=("arbitrary",) * n_axes,
        vmem_limit_bytes=min(vmem_mib * 1024 * 1024, V7X_VMEM_BYTES - 2 * 1024 * 1024),
        flags=flags,
    )


def _sigmoid(x):
    return 0.5 * jnp.tanh(0.5 * x) + 0.5


def _silu(x):
    return x * _sigmoid(x)


def _mm(a, b):
    return lax.dot_general(a, b, (((1,), (0,)), ((), ())), preferred_element_type=F32)


def _gelu_tanh(x):
    return 0.5 * x * (1.0 + jnp.tanh(0.7978845608028654 * (x + 0.044715 * (x * x * x))))


def _mod_row(i):
    return jnp.maximum(i - (CTX_MT - 1), 0)


def _mod_kernel(c_ref, w_ref, b_ref, o_ref):
    a = _silu(c_ref[...]).astype(BF16)
    o_ref[...] = jnp.dot(a, w_ref[...].astype(BF16), preferred_element_type=F32) + b_ref[...]


def _modulation(cond, w_mod, b_mod):
    n_out = w_mod.shape[-1]
    return pl.pallas_call(
        _mod_kernel,
        out_shape=jax.ShapeDtypeStruct((DEPTH, N_COND, n_out), F32),
        grid=(DEPTH, n_out // TN_MOD),
        in_specs=[
            pl.BlockSpec((N_COND, D_MODEL), lambda l, j: (0, 0)),
            pl.BlockSpec((None, D_MODEL, TN_MOD), lambda l, j: (l, 0, j)),
            pl.BlockSpec((None, 1, TN_MOD), lambda l, j: (l, 0, j)),
        ],
        out_specs=pl.BlockSpec((None, N_COND, TN_MOD), lambda l, j: (l, 0, j)),
        compiler_params=_cparams(2, 40),
        name="modulation",
    )(cond, w_mod, b_mod.reshape(DEPTH, 1, n_out))


def _bias_kernel(rb_ref, o_ref, td_scr):
    base = (pl.program_id(0) * N_HEADS + pl.program_id(1)) * (BIAS_ROWS * BIAS_COLS)
    qi = lax.broadcasted_iota(jnp.int32, (GRID_W, LANES), 0)
    li = lax.broadcasted_iota(jnp.int32, (GRID_W, LANES), 1)
    diff = jnp.clip((li & (GRID_W - 1)) - qi, -(WIN_COLS - 1), WIN_COLS - 1) + (WIN_COLS - 1)

    def body(i, carry):
        acc = jnp.zeros((GRID_W, LANES), F32)
        for d in range(BIAS_COLS):
            acc = jnp.where(diff == d, rb_ref[base + i * BIAS_COLS + d], acc)
        td_scr[i] = acc
        return carry

    lax.fori_loop(0, BIAS_ROWS, body, 0)
    td_scr[BIAS_ROWS] = jnp.zeros((GRID_W, LANES), F32)
    left = li < GRID_W
    for i in range(BIAS_ROWS):
        o_ref[i] = jnp.where(left, td_scr[i], td_scr[i + 1])


def _bias_table(na_bias):
    return pl.pallas_call(
        _bias_kernel,
        out_shape=jax.ShapeDtypeStruct((DEPTH, N_HEADS, BIAS_ROWS, GRID_W, LANES), F32),
        grid=(DEPTH, N_HEADS),
        in_specs=[pl.BlockSpec(memory_space=pltpu.SMEM)],
        out_specs=pl.BlockSpec((None, None, BIAS_ROWS, GRID_W, LANES), lambda l, h: (l, h, 0, 0, 0)),
        scratch_shapes=[pltpu.VMEM((BIAS_ROWS + 1, GRID_W, LANES), F32)],
        compiler_params=_cparams(2, 16),
        name="bias_table",
    )(na_bias.reshape(-1))


def _norm_mod(x_ref, g_ref, sc_ref, sh_ref, h_scr, row0=0):
    gs = g_ref[...] * (1.0 + sc_ref[...])
    shift = sh_ref[...]

    def body(c, carry):
        r0 = pl.multiple_of(row0 + c * NORM_ROWS, NORM_ROWS)
        x = x_ref[pl.ds(r0, NORM_ROWS), :]
        ms = jnp.mean(x * x, axis=-1, keepdims=True)
        h_scr[pl.ds(r0, NORM_ROWS), :] = (x * lax.rsqrt(ms + EPS) * gs + shift).astype(BF16)
        return carry

    lax.fori_loop(0, TM // NORM_ROWS, body, 0)


def _mod_spec(chunk):
    return pl.BlockSpec((None, None, 1, D_MODEL), lambda i, j, l: (l[0], _mod_row(i), 0, chunk))


def _mod_spec2(chunk, half):
    return pl.BlockSpec((None, None, 1, D_MODEL), lambda i, j, l: (l[0], _mod_row(2 * i + half), 0, chunk))


def _inproj_kernel(l_ref, x_ref, g_ref, sh0_ref, sc0_ref, sh1_ref, sc1_ref, w_ref, o_ref, h_scr):
    @pl.when(pl.program_id(1) == 0)
    def _():
        _norm_mod(x_ref, g_ref, sc0_ref, sh0_ref, h_scr, 0)
        _norm_mod(x_ref, g_ref, sc1_ref, sh1_ref, h_scr, TM)

    o_ref[...] = _mm(h_scr[...], w_ref[...])


def _inproj(l, x, ln_g, mod4, w_in):
    return pl.pallas_call(
        _inproj_kernel,
        out_shape=jax.ShapeDtypeStruct((T_ALL, D_IN), F32),
        grid_spec=pltpu.PrefetchScalarGridSpec(
            num_scalar_prefetch=1,
            grid=(N_MT2, D_IN // TN_IN),
            in_specs=[
                pl.BlockSpec((TM2, D_MODEL), lambda i, j, l: (i, 0)),
                pl.BlockSpec((None, 1, D_MODEL), lambda i, j, l: (l[0], 0, 0)),
                _mod_spec2(0, 0),
                _mod_spec2(1, 0),
                _mod_spec2(0, 1),
                _mod_spec2(1, 1),
                pl.BlockSpec((None, D_MODEL, TN_IN), lambda i, j, l: (l[0], 0, j)),
            ],
            out_specs=pl.BlockSpec((TM2, TN_IN), lambda i, j, l: (i, j)),
            scratch_shapes=[pltpu.VMEM((TM2, D_MODEL), BF16)],
        ),
        compiler_params=_cparams(2, 62),
        name="inproj",
    )(l, x, ln_g, mod4, mod4, mod4, mod4, w_in)


def _dot_nt(a, b):
    return lax.dot_general(a, b, (((1,), (1,)), ((), ())), preferred_element_type=F32)


def _att_ctx_kernel(l_ref, q_ref, k_ref, v_ref, ks_in_ref, vs_in_ref, o_ref, nk_ref, nv_ref):
    del l_ref, ks_in_ref, vs_in_ref
    for h in range(N_HEADS):
        sl = slice(h * HEAD_DIM, (h + 1) * HEAD_DIM)
        kf = k_ref[:, sl]
        vf = v_ref[:, sl]
        nk_ref[pl.ds(h, SEQ, stride=N_HEADS), :] = kf
        nv_ref[pl.ds(h, SEQ, stride=N_HEADS), :] = vf
        q = q_ref[:, sl].astype(BF16)
        k = kf.astype(BF16)
        v = vf.astype(BF16)
        s = _dot_nt(q, k) * ATT_SCALE
        p = jnp.exp(s - jnp.max(s, axis=-1, keepdims=True))
        denom = jnp.sum(p, axis=-1, keepdims=True)
        o = jnp.dot(p.astype(BF16), v, preferred_element_type=F32)
        o_ref[:, sl] = (o / denom).astype(o_ref.dtype)


def _att_ctx(l, z, ks, vs):
    kv_shape = jax.ShapeDtypeStruct((BATCH, DEPTH, SEQ * N_HEADS, HEAD_DIM), F32)
    kv_spec = pl.BlockSpec((None, None, SEQ * N_HEADS, HEAD_DIM), lambda b, l: (b, l[0], 0, 0))
    return pl.pallas_call(
        _att_ctx_kernel,
        out_shape=(jax.ShapeDtypeStruct((T_ALL, D_ATT), BF16), kv_shape, kv_shape),
        grid_spec=pltpu.PrefetchScalarGridSpec(
            num_scalar_prefetch=1,
            grid=(BATCH,),
            in_specs=[
                pl.BlockSpec((SEQ, D_ATT), lambda b, l: (b, 0)),
                pl.BlockSpec((SEQ, D_ATT), lambda b, l: (b, 1)),
                pl.BlockSpec((SEQ, D_ATT), lambda b, l: (b, 2)),
                pl.BlockSpec(memory_space=pl.ANY),
                pl.BlockSpec(memory_space=pl.ANY),
            ],
            out_specs=(pl.BlockSpec((SEQ, D_ATT), lambda b, l: (b, 0)), kv_spec, kv_spec),
        ),
        input_output_aliases={4: 1, 5: 2},
        compiler_params=_cparams(1, 32),
        name="att_ctx",
    )(l, z, z, z, ks, vs)


Q_ROWS = 4
_KEY_ROWS = ((0, 8), (0, 12), (4, 16), (8, 16))


def _win_start(rq):
    return min(max(rq - WIN_ROWS // 2, 0), GRID_R - WIN_ROWS)


def _att_lat_kernel(l_ref, q_ref, k_ref, v_ref, ck_ref, cv_ref, tt_ref, att_in_ref, o_ref):
    del l_ref, att_in_ref
    qi = lax.broadcasted_iota(jnp.int32, (GRID_W, LANES), 0)
    li = lax.broadcasted_iota(jnp.int32, (GRID_W, LANES), 1)
    kc = li & (GRID_W - 1)
    cs = jnp.clip(qi - WIN_COLS // 2, 0, GRID_W - WIN_COLS)
    in_cols = (kc >= cs) & (kc < cs + WIN_COLS)
    left = li < GRID_W
    mask_both = in_cols
    mask_left = in_cols & left
    mask_right = in_cols & jnp.logical_not(left)
    neg = jnp.full((GRID_W, LANES), NEG_INF, F32)

    kb = k_ref[...].astype(BF16)
    vb = v_ref[...].astype(BF16)
    head = pl.program_id(1)
    ckb = ck_ref[pl.ds(head, PAST_LEN, stride=N_HEADS), :].astype(BF16)
    cvb = cv_ref[pl.ds(head, PAST_LEN, stride=N_HEADS), :].astype(BF16)
    tq = Q_ROWS * GRID_W
    n_tiles = GRID_R // Q_ROWS

    def scores(t):
        lo, hi = _KEY_ROWS[t]
        q = q_ref[t * tq:(t + 1) * tq, :].astype(BF16)
        return _dot_nt(q, kb[lo * GRID_W:hi * GRID_W, :]), _dot_nt(q, ckb)

    nxt = scores(0)
    for t in range(n_tiles):
        lo, hi = _KEY_ROWS[t]
        s_w, s_c = nxt
        if t + 1 < n_tiles:
            nxt = scores(t + 1)
        s_c = s_c * ATT_SCALE
        rows = []
        for rl in range(Q_ROWS):
            rq = t * Q_ROWS + rl
            start = _win_start(rq)
            blks = []
            for m in range(lo // 2, hi // 2):
                v0 = start <= 2 * m < start + WIN_ROWS
                v1 = start <= 2 * m + 1 < start + WIN_ROWS
                if not (v0 or v1):
                    blks.append(neg)
                    continue
                i = 2 * m - rq + WIN_ROWS - 1
                assert 0 <= i < BIAS_ROWS
                mask = mask_both if (v0 and v1) else (mask_left if v0 else mask_right)
                c0 = (2 * m - lo) * GRID_W
                sb = s_w[rl * GRID_W:(rl + 1) * GRID_W, c0:c0 + LANES]
                blks.append(jnp.where(mask, sb * ATT_SCALE + tt_ref[i], NEG_INF))
            rows.append(jnp.concatenate(blks, axis=1))
        sw = jnp.concatenate(rows, axis=0)
        mx = jnp.maximum(jnp.max(sw, axis=-1, keepdims=True), jnp.max(s_c, axis=-1, keepdims=True))
        pw = jnp.exp(sw - mx)
        pc = jnp.exp(s_c - mx)
        denom = jnp.sum(pw, axis=-1, keepdims=True) + jnp.sum(pc, axis=-1, keepdims=True)
        o = jnp.dot(pw.astype(BF16), vb[lo * GRID_W:hi * GRID_W, :], preferred_element_type=F32)
        o = o + jnp.dot(pc.astype(BF16), cvb, preferred_element_type=F32)
        o_ref[t * tq:(t + 1) * tq, :] = (o / denom).astype(o_ref.dtype)


def _att_lat(l, z, cache_k, cache_v, tt, att):
    rb0 = T_CTX // DEC_SEQ
    hb = D_ATT // HEAD_DIM
    return pl.pallas_call(
        _att_lat_kernel,
        out_shape=jax.ShapeDtypeStruct((T_ALL, D_ATT), BF16),
        grid_spec=pltpu.PrefetchScalarGridSpec(
            num_scalar_prefetch=1,
            grid=(DEC_BATCH, N_HEADS),
            in_specs=[
                pl.BlockSpec((DEC_SEQ, HEAD_DIM), lambda b, h, l: (rb0 + b, h)),
                pl.BlockSpec((DEC_SEQ, HEAD_DIM), lambda b, h, l: (rb0 + b, hb + h)),
                pl.BlockSpec((DEC_SEQ, HEAD_DIM), lambda b, h, l: (rb0 + b, 2 * hb + h)),
                pl.BlockSpec((None, None, PAST_LEN * N_HEADS, HEAD_DIM), lambda b, h, l: (b, l[0], 0, 0)),
                pl.BlockSpec((None, None, PAST_LEN * N_HEADS, HEAD_DIM), lambda b, h, l: (b, l[0], 0, 0)),
                pl.BlockSpec((None, None, BIAS_ROWS, GRID_W, LANES), lambda b, h, l: (l[0], h, 0, 0, 0)),
                pl.BlockSpec(memory_space=pl.ANY),
            ],
            out_specs=pl.BlockSpec((DEC_SEQ, HEAD_DIM), lambda b, h, l: (rb0 + b, h)),
        ),
        input_output_aliases={7: 0},
        compiler_params=_cparams(2, 32),
        name="att_lat",
    )(l, z, z, z, cache_k, cache_v, tt, att)


CV_PAD = 16
CV_ROWS = 64


def _convmod_kernel(l_ref, a_ref, g_ref, w_ref, b_ref, lng_ref, lnb_ref, *rest):
    del l_ref
    o_ref, upad, ush = rest[-3:]
    s_len = a_ref.shape[0]
    n_sh = s_len + CV_PAD + SUBLANES
    upad[0:CV_PAD, :] = jnp.zeros((CV_PAD, D_CONV), F32)
    upad[CV_PAD + s_len:, :] = jnp.zeros((CV_PAD, D_CONV), F32)
    upad[CV_PAD:CV_PAD + s_len, :] = a_ref[...] * _sigmoid(g_ref[...])
    for j in range(SUBLANES):
        ush[j, 0:n_sh, :] = upad[j:j + n_sh, :]
    bias = b_ref[...]
    ln_g = lng_ref[...]
    ln_b = lnb_ref[...]

    def chunk(c, carry):
        r0 = pl.multiple_of(c * CV_ROWS, CV_ROWS)
        acc = jnp.broadcast_to(bias, (CV_ROWS, D_CONV))
        for k in range(CONV_WIDTH):
            m, j = divmod(k + CV_PAD - CONV_WIDTH // 2, SUBLANES)
            acc = acc + w_ref[k:k + 1, :] * ush[j, pl.ds(r0 + SUBLANES * m, CV_ROWS), :]
        mu = jnp.mean(acc, axis=-1, keepdims=True)
        xc = acc - mu
        var = jnp.mean(xc * xc, axis=-1, keepdims=True)
        y = xc * lax.rsqrt(var + EPS) * ln_g + ln_b
        o_ref[pl.ds(r0, CV_ROWS), :] = _silu(y).astype(o_ref.dtype)
        return carry

    lax.fori_loop(0, s_len // CV_ROWS, chunk, 0)


def _convmod(l, z, cv_w, cv_b, ln_g, ln_b, s_len, n_seq, row_block0, prev=None):
    cb = (3 * D_ATT) // D_CONV
    in_specs = [
        pl.BlockSpec((s_len, D_CONV), lambda b, l: (row_block0 + b, cb)),
        pl.BlockSpec((s_len, D_CONV), lambda b, l: (row_block0 + b, cb + 1)),
        pl.BlockSpec((None, CONV_WIDTH, D_CONV), lambda b, l: (l[0], 0, 0)),
        pl.BlockSpec((None, 1, D_CONV), lambda b, l: (l[0], 0, 0)),
        pl.BlockSpec((None, 1, D_CONV), lambda b, l: (l[0], 0, 0)),
        pl.BlockSpec((None, 1, D_CONV), lambda b, l: (l[0], 0, 0)),
    ]
    args = [l, z, z, cv_w, cv_b, ln_g, ln_b]
    aliases = {}
    if prev is not None:
        in_specs.append(pl.BlockSpec(memory_space=pl.ANY))
        args.append(prev)
        aliases = {len(args) - 1: 0}
    return pl.pallas_call(
        _convmod_kernel,
        out_shape=jax.ShapeDtypeStruct((T_ALL, D_CONV), BF16),
        grid_spec=pltpu.PrefetchScalarGridSpec(
            num_scalar_prefetch=1,
            grid=(n_seq,),
            in_specs=in_specs,
            out_specs=pl.BlockSpec((s_len, D_CONV), lambda b, l: (row_block0 + b, 0)),
            scratch_shapes=[
                pltpu.VMEM((s_len + 2 * CV_PAD, D_CONV), F32),
                pltpu.VMEM((SUBLANES, s_len + CV_PAD + SUBLANES, D_CONV), F32),
            ],
        ),
        input_output_aliases=aliases,
        compiler_params=_cparams(1, 48),
        name=f"convmod_{s_len}",
    )(*args)


LRU_PAD = 8
LRU_HALF = D_LRU // 2
LRU_CHUNK = 256


def _softplus(x):
    return jnp.maximum(x, 0.0) + jnp.log1p(jnp.exp(-jnp.abs(x)))


def _lru_kernel(l_ref, x_ref, g_ref, h0_ref, cw_ref, cb_ref, wa_ref, ba_ref, wi_ref, bi_ref, lam_ref, *rest):
    del l_ref
    y_ref, hl_ref, xpad, a_scr, b_scr, h_scr = rest[-6:]
    s_len = x_ref.shape[0]
    n_blk = s_len // SUBLANES
    n_lt = D_LRU // LANES

    xpad[0:LRU_PAD, :] = jnp.zeros((LRU_PAD, D_LRU), F32)
    xpad[LRU_PAD + s_len:, :] = jnp.zeros((LRU_PAD, D_LRU), F32)
    xpad[LRU_PAD:LRU_PAD + s_len, :] = x_ref[...]
    cb = cb_ref[...]
    decay = [-LRU_C * _softplus(-lam_ref[d:d + 1, :]) for d in range(2)]

    for c in range(s_len // LRU_CHUNK):
        r0 = c * LRU_CHUNK
        xc = jnp.broadcast_to(cb, (LRU_CHUNK, D_LRU))
        for k in range(LRU_CONV_WIDTH):
            off = LRU_PAD + r0 + k - LRU_CONV_WIDTH // 2
            xc = xc + cw_ref[k:k + 1, :] * xpad[off:off + LRU_CHUNK, :]
        xb = xc.astype(BF16)
        for d in range(2):
            ga = jnp.concatenate(
                [jnp.dot(xb[:, hh * LRU_HALF:(hh + 1) * LRU_HALF], wa_ref[d, hh], preferred_element_type=F32)
                 for hh in range(2)], axis=1) + ba_ref[d:d + 1, :]
            gi = jnp.concatenate(
                [jnp.dot(xb[:, hh * LRU_HALF:(hh + 1) * LRU_HALF], wi_ref[d, hh], preferred_element_type=F32)
                 for hh in range(2)], axis=1) + bi_ref[d:d + 1, :]
            log_a = _sigmoid(ga) * decay[d]
            th = jnp.tanh(log_a)
            gain = jnp.sqrt(-2.0 * th / (1.0 - th))
            a_scr[d, r0:r0 + LRU_CHUNK, :] = jnp.exp(log_a)
            b_scr[d, r0:r0 + LRU_CHUNK, :] = gain * (_sigmoid(gi) * xc)

    row = lax.broadcasted_iota(jnp.int32, (SUBLANES, LANES), 0)

    def scan_block(it, carry):
        new = []
        for d in range(2):
            blk = it if d == 0 else n_blk - 1 - it
            r0 = pl.multiple_of(blk * SUBLANES, SUBLANES)
            for c in range(n_lt):
                ls = slice(c * LANES, (c + 1) * LANES)
                a = a_scr[d, pl.ds(r0, SUBLANES), ls]
                b = b_scr[d, pl.ds(r0, SUBLANES), ls]
                for s in (1, 2, 4):
                    shift = s if d == 0 else SUBLANES - s
                    valid = (row >= s) if d == 0 else (row < SUBLANES - s)
                    a_sh = jnp.where(valid, pltpu.roll(a, shift, 0), 1.0)
                    b_sh = jnp.where(valid, pltpu.roll(b, shift, 0), 0.0)
                    b = a * b_sh + b
                    a = a * a_sh
                h = a * carry[d * n_lt + c] + b
                h_scr[d, pl.ds(r0, SUBLANES), ls] = h
                last = h[SUBLANES - 1:SUBLANES, :] if d == 0 else h[0:1, :]
                new.append(jnp.broadcast_to(last, (SUBLANES, LANES)))
        return tuple(new)

    init = tuple(
        jnp.broadcast_to(h0_ref[d:d + 1, c * LANES:(c + 1) * LANES], (SUBLANES, LANES))
        for d in range(2) for c in range(n_lt))
    final = lax.fori_loop(0, n_blk, scan_block, init)
    for d in range(2):
        for c in range(n_lt):
            hl_ref[d:d + 1, c * LANES:(c + 1) * LANES] = final[d * n_lt + c][0:1, :]

    for c in range(s_len // LRU_CHUNK):
        rs = slice(c * LRU_CHUNK, (c + 1) * LRU_CHUNK)
        y = (h_scr[0, rs, :] + h_scr[1, rs, :]) * _gelu_tanh(g_ref[rs, :])
        y_ref[rs, :] = y.astype(y_ref.dtype)


def _lru(l, z, h0, p, s_len, n_seq, row_block0, prev=None):
    cb = (3 * D_ATT + 2 * D_CONV) // D_LRU
    in_specs = [
        pl.BlockSpec((s_len, D_LRU), lambda b, l: (row_block0 + b, cb)),
        pl.BlockSpec((s_len, D_LRU), lambda b, l: (row_block0 + b, cb + 1)),
        pl.BlockSpec((None, 2, D_LRU), lambda b, l: (b, 0, 0)),
        pl.BlockSpec((None, LRU_CONV_WIDTH, D_LRU), lambda b, l: (l[0], 0, 0)),
        pl.BlockSpec((None, 1, D_LRU), lambda b, l: (l[0], 0, 0)),
        pl.BlockSpec((None, 2, 2, LRU_HALF, LRU_HALF), lambda b, l: (l[0], 0, 0, 0, 0)),
        pl.BlockSpec((None, 2, D_LRU), lambda b, l: (l[0], 0, 0)),
        pl.BlockSpec((None, 2, 2, LRU_HALF, LRU_HALF), lambda b, l: (l[0], 0, 0, 0, 0)),
        pl.BlockSpec((None, 2, D_LRU), lambda b, l: (l[0], 0, 0)),
        pl.BlockSpec((None, 2, D_LRU), lambda b, l: (l[0], 0, 0)),
    ]
    args = [l, z, z, h0, p["lru_conv_w"], p["lru_conv_b"], p["wa_dense"], p["lru_ba"], p["wi_dense"],
            p["lru_bi"], p["lru_lam"]]
    aliases = {}
    if prev is not None:
        in_specs.append(pl.BlockSpec(memory_space=pl.ANY))
        args.append(prev)
        aliases = {len(args) - 1: 0}
    return pl.pallas_call(
        _lru_kernel,
        out_shape=(jax.ShapeDtypeStruct((T_ALL, D_LRU), BF16),
                   jax.ShapeDtypeStruct((n_seq, 2, D_LRU), F32)),
        grid_spec=pltpu.PrefetchScalarGridSpec(
            num_scalar_prefetch=1,
            grid=(n_seq,),
            in_specs=in_specs,
            out_specs=(pl.BlockSpec((s_len, D_LRU), lambda b, l: (row_block0 + b, 0)),
                       pl.BlockSpec((None, 2, D_LRU), lambda b, l: (b, 0, 0))),
            scratch_shapes=[
                pltpu.VMEM((s_len + 2 * LRU_PAD, D_LRU), F32),
                pltpu.VMEM((2, s_len, D_LRU), F32),
                pltpu.VMEM((2, s_len, D_LRU), F32),
                pltpu.VMEM((2, s_len, D_LRU), F32),
            ],
        ),
        input_output_aliases=aliases,
        compiler_params=_cparams(1, 48),
        name=f"rglru_{s_len}",
    )(*args)


TM_OUT = 512


def _outproj_kernel(l_ref, att_ref, cv_ref, rec_ref, w_ref, x_ref, gate_ref, g_ref, sh_ref, sc_ref,
                    o_ref, h_ref):
    del l_ref
    y = jnp.dot(att_ref[...], w_ref[0:D_ATT, :], preferred_element_type=F32)
    y = y + jnp.dot(cv_ref[...], w_ref[D_ATT:D_ATT + D_CONV, :], preferred_element_type=F32)
    y = y + jnp.dot(rec_ref[...], w_ref[D_ATT + D_CONV:D_MIX, :], preferred_element_type=F32)
    o_ref[...] = x_ref[...] + gate_ref[...] * y
    gs = g_ref[...] * (1.0 + sc_ref[...])
    shift = sh_ref[...]
    for c in range(TM_OUT // NORM_ROWS):
        rs = slice(c * NORM_ROWS, (c + 1) * NORM_ROWS)
        xn = o_ref[rs, :]
        ms = jnp.mean(xn * xn, axis=-1, keepdims=True)
        h_ref[rs, :] = (xn * lax.rsqrt(ms + EPS) * gs + shift).astype(BF16)


def _outproj(l, att, cv, rec, w_out, x, ln_g, mod4):
    def mod_spec(chunk):
        return pl.BlockSpec((None, None, 1, D_MODEL),
                            lambda t, l: (l[0], _mod_row(t // (TM // TM_OUT)), 0, chunk))

    return pl.pallas_call(
        _outproj_kernel,
        out_shape=(jax.ShapeDtypeStruct((T_ALL, D_MODEL), F32),
                   jax.ShapeDtypeStruct((T_ALL, D_MODEL), BF16)),
        grid_spec=pltpu.PrefetchScalarGridSpec(
            num_scalar_prefetch=1,
            grid=(T_ALL // TM_OUT,),
            in_specs=[
                pl.BlockSpec((TM_OUT, D_ATT), lambda t, l: (t, 0)),
                pl.BlockSpec((TM_OUT, D_CONV), lambda t, l: (t, 0)),
                pl.BlockSpec((TM_OUT, D_LRU), lambda t, l: (t, 0)),
                pl.BlockSpec((None, D_MIX, D_MODEL), lambda t, l: (l[0], 0, 0)),
                pl.BlockSpec((TM_OUT, D_MODEL), lambda t, l: (t, 0)),
                mod_spec(2),
                pl.BlockSpec((None, 1, D_MODEL), lambda t, l: (l[0], 0, 0)),
                mod_spec(3),
                mod_spec(4),
            ],
            out_specs=(pl.BlockSpec((TM_OUT, D_MODEL), lambda t, l: (t, 0)),
                       pl.BlockSpec((TM_OUT, D_MODEL), lambda t, l: (t, 0))),
        ),
        compiler_params=_cparams(1, 48),
        name="outproj",
    )(l, att, cv, rec, w_out, x, mod4, ln_g, mod4, mod4)


TN_DOWN = 512
FFN_SPLIT = 2
FFN_HALO = 16
FFN_AHEAD = 3


def _ffn_kernel(l_ref, x_ref, h_scr, gate_ref, wug_ref, wuv_ref, cwg_ref, cwv_ref,
                cbg_ref, cbv_ref, wd_ref, o_ref, mprev_scr, mnext_scr):
    del l_ref
    i = pl.program_id(0)
    j = pl.program_id(1)

    @pl.when(j == 0)
    def _():
        o_ref[...] = jnp.zeros_like(o_ref)
        seq_len = jnp.where(i < CTX_MT, SEQ, DEC_SEQ)
        pos = lax.broadcasted_iota(jnp.int32, (TM, LANES), 0) & (seq_len - 1)
        mprev_scr[...] = jnp.where(pos == 0, 0.0, 1.0)
        mnext_scr[...] = jnp.where(pos == seq_len - 1, 0.0, 1.0)

    rows = TM // FFN_SPLIT
    def bounds(s):
        return max(s * rows - FFN_HALO, 0), min((s + 1) * rows + FFN_HALO, TM)

    def up(s):
        lo, hi = bounds(s)
        h = h_scr[lo:hi, :]
        return (_mm(h, wug_ref[...]), _mm(h, wuv_ref[...]))

    def conv(u, s, cw_ref, cb_ref):
        lo, hi = bounds(s)
        off = s * rows - lo
        prev = pltpu.roll(u, 1, 0) * jnp.tile(mprev_scr[lo:hi, :], (1, TF // LANES))
        nxt = pltpu.roll(u, hi - lo - 1, 0) * jnp.tile(mnext_scr[lo:hi, :], (1, TF // LANES))
        c = cw_ref[0:1, :] * prev + cw_ref[1:2, :] * u + cw_ref[2:3, :] * nxt + cb_ref[...]
        return c[off:off + rows, :]

    pending = [up(s) for s in range(min(FFN_AHEAD, FFN_SPLIT))]
    for s in range(FFN_SPLIT):
        if s + FFN_AHEAD < FFN_SPLIT:
            pending.append(up(s + FFN_AHEAD))
        u_cur = pending.pop(0)
        ug = conv(u_cur[0], s, cwg_ref, cbg_ref)
        uv = conv(u_cur[1], s, cwv_ref, cbv_ref)
        act = (_silu(ug) * uv).astype(BF16)
        rs = slice(s * rows, (s + 1) * rows)
        for n in range(D_MODEL // TN_DOWN):
            ns = slice(n * TN_DOWN, (n + 1) * TN_DOWN)
            o_ref[rs, ns] = _mm(act, wd_ref[:, ns]) + o_ref[rs, ns]

    @pl.when(j == N_FT - 1)
    def _():
        o_ref[...] = x_ref[...] + gate_ref[...] * o_ref[...]


def _ffn(l, x, h2, mod4, ffn_up, conv_w, conv_b, ffn_down):
    return pl.pallas_call(
        _ffn_kernel,
        out_shape=jax.ShapeDtypeStruct((T_ALL, D_MODEL), F32),
        grid_spec=pltpu.PrefetchScalarGridSpec(
            num_scalar_prefetch=1,
            grid=(N_MT, N_FT),
            in_specs=[
                pl.BlockSpec((TM, D_MODEL), lambda i, j, l: (i, 0), pipeline_mode=pl.Buffered(1)),
                pl.BlockSpec((TM, D_MODEL), lambda i, j, l: (i, 0), pipeline_mode=pl.Buffered(1)),
                _mod_spec(5),
                pl.BlockSpec((None, D_MODEL, TF), lambda i, j, l: (l[0], 0, j)),
                pl.BlockSpec((None, D_MODEL, TF), lambda i, j, l: (l[0], 0, N_FT + j)),
                pl.BlockSpec((None, 3, TF), lambda i, j, l: (l[0], 0, j)),
                pl.BlockSpec((None, 3, TF), lambda i, j, l: (l[0], 0, N_FT + j)),
                pl.BlockSpec((None, 1, TF), lambda i, j, l: (l[0], 0, j)),
                pl.BlockSpec((None, 1, TF), lambda i, j, l: (l[0], 0, N_FT + j)),
                pl.BlockSpec((None, TF, D_MODEL), lambda i, j, l: (l[0], j, 0)),
            ],
            out_specs=pl.BlockSpec((TM, D_MODEL), lambda i, j, l: (i, 0)),
            scratch_shapes=[
                pltpu.VMEM((TM, LANES), F32),
                pltpu.VMEM((TM, LANES), F32),
            ],
        ),
        compiler_params=_cparams(2, 62),
        name="convffn",
    )(l, x, h2, mod4, ffn_up, ffn_up, conv_w, conv_w, conv_b, conv_b, ffn_down)


def _final_kernel(x_ref, g_ref, o_ref):
    g = g_ref[...]
    for c in range(TM_FINAL // NORM_ROWS):
        rs = slice(c * NORM_ROWS, (c + 1) * NORM_ROWS)
        x = x_ref[rs, :]
        ms = jnp.mean(x * x, axis=-1, keepdims=True)
        o_ref[rs, :] = x * lax.rsqrt(ms + EPS) * g


def _final_norm(x, g, row0, n_rows):
    tile0 = row0 // TM_FINAL
    return pl.pallas_call(
        _final_kernel,
        out_shape=jax.ShapeDtypeStruct((n_rows, D_MODEL), F32),
        grid=(n_rows // TM_FINAL,),
        in_specs=[
            pl.BlockSpec((TM_FINAL, D_MODEL), lambda i: (tile0 + i, 0)),
            pl.BlockSpec((1, D_MODEL), lambda i: (0, 0)),
        ],
        out_specs=pl.BlockSpec((TM_FINAL, D_MODEL), lambda i: (i, 0)),
        compiler_params=_cparams(1, 32),
        name="final_norm",
    )(x, g)


def _block_diag_halves(w):
    nb = LRU_BLOCKS // 2
    w6 = w.reshape(DEPTH, 2, 2, nb, LRU_BW, LRU_BW)
    eye = jnp.eye(nb, dtype=w.dtype)
    dense = jnp.einsum("ldhjab,jk->ldhjakb", w6, eye)
    dense = dense.reshape(DEPTH, 2, 2, LRU_HALF, LRU_HALF).astype(BF16)
    return lax.optimization_barrier(dense)


def kernel(x_prompt, x_sample, cache_k, cache_v, state_lru, c, c_ctx, ln1_g, w_mod, b_mod, w_in, na_bias,
           cv_w, cv_b, cv_ln_g, cv_ln_b, lru_conv_w, lru_conv_b, lru_wa, lru_ba, lru_wi, lru_bi, lru_lam,
           w_out, ln2_g, ffn_up, ffn_conv_w, ffn_conv_b, ffn_down, final_g):
    x = jnp.concatenate([x_prompt.reshape(T_CTX, D_MODEL), x_sample.reshape(T_LAT, D_MODEL)], axis=0)
    cond = jnp.concatenate([c_ctx[None, :], c, jnp.zeros((N_COND - 1 - DEC_BATCH, D_MODEL), F32)], axis=0)
    mod4 = _modulation(cond, w_mod, b_mod).reshape(DEPTH, N_COND, 1, 6 * D_MODEL)
    tt = _bias_table(na_bias)
    ck = cache_k.reshape(DEC_BATCH, DEPTH, PAST_LEN * N_HEADS, HEAD_DIM)
    cvv = cache_v.reshape(DEC_BATCH, DEPTH, PAST_LEN * N_HEADS, HEAD_DIM)
    state = jnp.swapaxes(state_lru, 0, 1)
    h0_ctx = jnp.zeros((BATCH, 2, D_LRU), F32)
    lru_p = {
        "lru_conv_w": lru_conv_w,
        "lru_conv_b": lru_conv_b.reshape(DEPTH, 1, D_LRU),
        "wa_dense": _block_diag_halves(lru_wa),
        "lru_ba": lru_ba,
        "wi_dense": _block_diag_halves(lru_wi),
        "lru_bi": lru_bi,
        "lru_lam": lru_lam,
    }
    ln1 = ln1_g.reshape(DEPTH, 1, D_MODEL)
    ln2 = ln2_g.reshape(DEPTH, 1, D_MODEL)
    cvb = cv_b.reshape(DEPTH, 1, D_CONV)
    cvg = cv_ln_g.reshape(DEPTH, 1, D_CONV)
    cvlb = cv_ln_b.reshape(DEPTH, 1, D_CONV)
    fcb = ffn_conv_b.reshape(DEPTH, 1, 2 * D_FF)
    lat_sb = T_CTX // DEC_SEQ
    w_out = w_out.astype(BF16)

    def layer(carry, xs):
        x, ks, vs, hs = carry
        li, h0_lat = xs
        l = jnp.reshape(li, (1,))
        z = _inproj(l, x, ln1, mod4, w_in)
        att, ks, vs = _att_ctx(l, z, ks, vs)
        att = _att_lat(l, z, ck, cvv, tt, att)
        cvo = _convmod(l, z, cv_w, cvb, cvg, cvlb, SEQ, BATCH, 0)
        cvo = _convmod(l, z, cv_w, cvb, cvg, cvlb, DEC_SEQ, DEC_BATCH, lat_sb, prev=cvo)
        rec, h_last = _lru(l, z, h0_ctx, lru_p, SEQ, BATCH, 0)
        rec, _ = _lru(l, z, h0_lat, lru_p, DEC_SEQ, DEC_BATCH, lat_sb, prev=rec)
        x, h2 = _outproj(l, att, cvo, rec, w_out, x, ln2, mod4)
        x = _ffn(l, x, h2, mod4, ffn_up, ffn_conv_w, fcb, ffn_down)
        hs = lax.dynamic_update_slice(hs, h_last[:, None], (0, li, 0, 0))
        return (x, ks, vs, hs), None

    init = (x,
            jnp.zeros((BATCH, DEPTH, SEQ * N_HEADS, HEAD_DIM), F32),
            jnp.zeros((BATCH, DEPTH, SEQ * N_HEADS, HEAD_DIM), F32),
            jnp.zeros((BATCH, DEPTH, 2, D_LRU), F32))
    (x, ks, vs, hs), _ = lax.scan(layer, init, (jnp.arange(DEPTH, dtype=jnp.int32), state))

    g = final_g.reshape(1, D_MODEL)
    y_prompt = _final_norm(x, g, 0, T_CTX).reshape(BATCH, SEQ, D_MODEL)
    y_sample = _final_norm(x, g, T_CTX, T_LAT).reshape(DEC_BATCH, DEC_SEQ, D_MODEL)
    new_k = ks.reshape(BATCH, DEPTH, SEQ, N_HEADS, HEAD_DIM)
    new_v = vs.reshape(BATCH, DEPTH, SEQ, N_HEADS, HEAD_DIM)
    return (y_prompt, y_sample, new_k, new_v, hs)
```

```python
import functools

import jax
import jax.numpy as jnp
from jax import lax
from jax.experimental import pallas as pl
from jax.experimental.pallas import tpu as pltpu

F32 = jnp.float32
BF16 = jnp.bfloat16

D_MODEL = 2048
BATCH = 16
SEQ = 256
DEPTH = 4
DEC_BATCH = 2
DEC_SEQ = 1024
PAST_LEN = 512
GRID_W = 64
GRID_R = DEC_SEQ // GRID_W
D_ATT = 1024
N_HEADS = 8
HEAD_DIM = 128
D_CONV = 512
D_LRU = 512
D_MIX = D_ATT + D_CONV + D_LRU
D_IN = 3 * D_ATT + 2 * D_CONV + 2 * D_LRU
WIN_ROWS = 8
WIN_COLS = 16
CONV_WIDTH = 31
LRU_CONV_WIDTH = 4
LRU_BLOCKS = 8
LRU_BW = D_LRU // LRU_BLOCKS
LRU_C = 8.0
D_FF = 5632
EPS = 1e-6
ATT_SCALE = HEAD_DIM ** -0.5
NEG_INF = -1e30

T_CTX = BATCH * SEQ
T_LAT = DEC_BATCH * DEC_SEQ
T_ALL = T_CTX + T_LAT

V7X_VMEM_BYTES = 64 * 1024 * 1024
SUBLANES = 8
LANES = 128

TM = 1024
N_MT = T_ALL // TM
CTX_MT = T_CTX // TM
TM2 = 2 * TM
N_MT2 = T_ALL // TM2
TM_FINAL = 256
TN_IN = 512
TF = 512
N_FT = D_FF // TF
TN_MOD = 1024
NORM_ROWS = 128
N_COND = 8
BIAS_ROWS = 2 * WIN_ROWS - 1
BIAS_COLS = 2 * WIN_COLS - 1


def _cparams(n_axes, vmem_mib, flags=None):
    return pltpu.CompilerParams(
        dimension_semantics=("arbitrary",) * n_axes,
        vmem_limit_bytes=min(vmem_mib * 1024 * 1024, V7X_VMEM_BYTES - 2 * 1024 * 1024),
        flags=flags,
    )


def _sigmoid(x):
    return 0.5 * jnp.tanh(0.5 * x) + 0.5


def _silu(x):
    return x * _sigmoid(x)


def _mm(a, b):
    return lax.dot_general(a, b, (((1,), (0,)), ((), ())), preferred_element_type=F32)


def _gelu_tanh(x):
    return 0.5 * x * (1.0 + jnp.tanh(0.7978845608028654 * (x + 0.044715 * (x * x * x))))


def _mod_row(i):
    return jnp.maximum(i - (CTX_MT - 1), 0)


def _mod_kernel(c_ref, w_ref, b_ref, o_ref):
    a = _silu(c_ref[...]).astype(BF16)
    o_ref[...] = jnp.dot(a, w_ref[...].astype(BF16), preferred_element_type=F32) + b_ref[...]


def _modulation(cond, w_mod, b_mod):
    n_out = w_mod.shape[-1]
    return pl.pallas_call(
        _mod_kernel,
        out_shape=jax.ShapeDtypeStruct((DEPTH, N_COND, n_out), F32),
        grid=(DEPTH, n_out // TN_MOD),
        in_specs=[
            pl.BlockSpec((N_COND, D_MODEL), lambda l, j: (0, 0)),
            pl.BlockSpec((None, D_MODEL, TN_MOD), lambda l, j: (l, 0, j)),
            pl.BlockSpec((None, 1, TN_MOD), lambda l, j: (l, 0, j)),
        ],
        out_specs=pl.BlockSpec((None, N_COND, TN_MOD), lambda l, j: (l, 0, j)),
        compiler_params=_cparams(2, 40),
        name="modulation",
    )(cond, w_mod, b_mod.reshape(DEPTH, 1, n_out))


def _bias_kernel(rb_ref, o_ref, td_scr):
    base = (pl.program_id(0) * N_HEADS + pl.program_id(1)) * (BIAS_ROWS * BIAS_COLS)
    qi = lax.broadcasted_iota(jnp.int32, (GRID_W, LANES), 0)
    li = lax.broadcasted_iota(jnp.int32, (GRID_W, LANES), 1)
    diff = jnp.clip((li & (GRID_W - 1)) - qi, -(WIN_COLS - 1), WIN_COLS - 1) + (WIN_COLS - 1)

    def body(i, carry):
        acc = jnp.zeros((GRID_W, LANES), F32)
        for d in range(BIAS_COLS):
            acc = jnp.where(diff == d, rb_ref[base + i * BIAS_COLS + d], acc)
        td_scr[i] = acc
        return carry

    lax.fori_loop(0, BIAS_ROWS, body, 0)
    td_scr[BIAS_ROWS] = jnp.zeros((GRID_W, LANES), F32)
    left = li < GRID_W
    for i in range(BIAS_ROWS):
        o_ref[i] = jnp.where(left, td_scr[i], td_scr[i + 1])


def _bias_table(na_bias):
    return pl.pallas_call(
        _bias_kernel,
        out_shape=jax.ShapeDtypeStruct((DEPTH, N_HEADS, BIAS_ROWS, GRID_W, LANES), F32),
        grid=(DEPTH, N_HEADS),
        in_specs=[pl.BlockSpec(memory_space=pltpu.SMEM)],
        out_specs=pl.BlockSpec((None, None, BIAS_ROWS, GRID_W, LANES), lambda l, h: (l, h, 0, 0, 0)),
        scratch_shapes=[pltpu.VMEM((BIAS_ROWS + 1, GRID_W, LANES), F32)],
        compiler_params=_cparams(2, 16),
        name="bias_table",
    )(na_bias.reshape(-1))


def _norm_mod(x_ref, g_ref, sc_ref, sh_ref, h_scr, row0=0):
    gs = g_ref[...] * (1.0 + sc_ref[...])
    shift = sh_ref[...]

    def body(c, carry):
        r0 = pl.multiple_of(row0 + c * NORM_ROWS, NORM_ROWS)
        x = x_ref[pl.ds(r0, NORM_ROWS), :]
        ms = jnp.mean(x * x, axis=-1, keepdims=True)
        h_scr[pl.ds(r0, NORM_ROWS), :] = (x * lax.rsqrt(ms + EPS) * gs + shift).astype(BF16)
        return carry

    lax.fori_loop(0, TM // NORM_ROWS, body, 0)


def _mod_spec(chunk):
    return pl.BlockSpec((None, None, 1, D_MODEL), lambda i, j, l: (l[0], _mod_row(i), 0, chunk))


def _mod_spec2(chunk, half):
    return pl.BlockSpec((None, None, 1, D_MODEL), lambda i, j, l: (l[0], _mod_row(2 * i + half), 0, chunk))


def _inproj_kernel(l_ref, x_ref, g_ref, sh0_ref, sc0_ref, sh1_ref, sc1_ref, w_ref, o_ref, h_scr):
    @pl.when(pl.program_id(1) == 0)
    def _():
        _norm_mod(x_ref, g_ref, sc0_ref, sh0_ref, h_scr, 0)
        _norm_mod(x_ref, g_ref, sc1_ref, sh1_ref, h_scr, TM)

    o_ref[...] = _mm(h_scr[...], w_ref[...])


def _inproj(l, x, ln_g, mod4, w_in):
    return pl.pallas_call(
        _inproj_kernel,
        out_shape=jax.ShapeDtypeStruct((T_ALL, D_IN), F32),
        grid_spec=pltpu.PrefetchScalarGridSpec(
            num_scalar_prefetch=1,
            grid=(N_MT2, D_IN // TN_IN),
            in_specs=[
                pl.BlockSpec((TM2, D_MODEL), lambda i, j, l: (i, 0)),
                pl.BlockSpec((None, 1, D_MODEL), lambda i, j, l: (l[0], 0, 0)),
                _mod_spec2(0, 0),
                _mod_spec2(1, 0),
                _mod_spec2(0, 1),
                _mod_spec2(1, 1),
                pl.BlockSpec((None, D_MODEL, TN_IN), lambda i, j, l: (l[0], 0, j)),
            ],
            out_specs=pl.BlockSpec((TM2, TN_IN), lambda i, j, l: (i, j)),
            scratch_shapes=[pltpu.VMEM((TM2, D_MODEL), BF16)],
        ),
        compiler_params=_cparams(2, 62),
        name="inproj",
    )(l, x, ln_g, mod4, mod4, mod4, mod4, w_in)


def _dot_nt(a, b):
    return lax.dot_general(a, b, (((1,), (1,)), ((), ())), preferred_element_type=F32)


def _att_ctx_kernel(l_ref, q_ref, k_ref, v_ref, ks_in_ref, vs_in_ref, o_ref, nk_ref, nv_ref):
    del l_ref, ks_in_ref, vs_in_ref
    for h in range(N_HEADS):
        sl = slice(h * HEAD_DIM, (h + 1) * HEAD_DIM)
        kf = k_ref[:, sl]
        vf = v_ref[:, sl]
        nk_ref[pl.ds(h, SEQ, stride=N_HEADS), :] = kf
        nv_ref[pl.ds(h, SEQ, stride=N_HEADS), :] = vf
        q = q_ref[:, sl].astype(BF16)
        k = kf.astype(BF16)
        v = vf.astype(BF16)
        s = _dot_nt(q, k) * ATT_SCALE
        p = jnp.exp(s - jnp.max(s, axis=-1, keepdims=True))
        denom = jnp.sum(p, axis=-1, keepdims=True)
        o = jnp.dot(p.astype(BF16), v, preferred_element_type=F32)
        o_ref[:, sl] = (o / denom).astype(o_ref.dtype)


def _att_ctx(l, z, ks, vs):
    kv_shape = jax.ShapeDtypeStruct((BATCH, DEPTH, SEQ * N_HEADS, HEAD_DIM), F32)
    kv_spec = pl.BlockSpec((None, None, SEQ * N_HEADS, HEAD_DIM), lambda b, l: (b, l[0], 0, 0))
    return pl.pallas_call(
        _att_ctx_kernel,
        out_shape=(jax.ShapeDtypeStruct((T_ALL, D_ATT), BF16), kv_shape, kv_shape),
        grid_spec=pltpu.PrefetchScalarGridSpec(
            num_scalar_prefetch=1,
            grid=(BATCH,),
            in_specs=[
                pl.BlockSpec((SEQ, D_ATT), lambda b, l: (b, 0)),
                pl.BlockSpec((SEQ, D_ATT), lambda b, l: (b, 1)),
                pl.BlockSpec((SEQ, D_ATT), lambda b, l: (b, 2)),
                pl.BlockSpec(memory_space=pl.ANY),
                pl.BlockSpec(memory_space=pl.ANY),
            ],
            out_specs=(pl.BlockSpec((SEQ, D_ATT), lambda b, l: (b, 0)), kv_spec, kv_spec),
        ),
        input_output_aliases={4: 1, 5: 2},
        compiler_params=_cparams(1, 32),
        name="att_ctx",
    )(l, z, z, z, ks, vs)


Q_ROWS = 4
_KEY_ROWS = ((0, 8), (0, 12), (4, 16), (8, 16))


def _win_start(rq):
    return min(max(rq - WIN_ROWS // 2, 0), GRID_R - WIN_ROWS)


def _att_lat_kernel(l_ref, q_ref, k_ref, v_ref, ck_ref, cv_ref, tt_ref, att_in_ref, o_ref):
    del l_ref, att_in_ref
    qi = lax.broadcasted_iota(jnp.int32, (GRID_W, LANES), 0)
    li = lax.broadcasted_iota(jnp.int32, (GRID_W, LANES), 1)
    kc = li & (GRID_W - 1)
    cs = jnp.clip(qi - WIN_COLS // 2, 0, GRID_W - WIN_COLS)
    in_cols = (kc >= cs) & (kc < cs + WIN_COLS)
    left = li < GRID_W
    mask_both = in_cols
    mask_left = in_cols & left
    mask_right = in_cols & jnp.logical_not(left)
    neg = jnp.full((GRID_W, LANES), NEG_INF, F32)

    kb = k_ref[...].astype(BF16)
    vb = v_ref[...].astype(BF16)
    head = pl.program_id(1)
    ckb = ck_ref[pl.ds(head, PAST_LEN, stride=N_HEADS), :].astype(BF16)
    cvb = cv_ref[pl.ds(head, PAST_LEN, stride=N_HEADS), :].astype(BF16)
    tq = Q_ROWS * GRID_W
    n_tiles = GRID_R // Q_ROWS

    def scores(t):
        lo, hi = _KEY_ROWS[t]
        q = q_ref[t * tq:(t + 1) * tq, :].astype(BF16)
        return _dot_nt(q, kb[lo * GRID_W:hi * GRID_W, :]), _dot_nt(q, ckb)

    nxt = scores(0)
    for t in range(n_tiles):
        lo, hi = _KEY_ROWS[t]
        s_w, s_c = nxt
        if t + 1 < n_tiles:
            nxt = scores(t + 1)
        s_c = s_c * ATT_SCALE
        rows = []
        for rl in range(Q_ROWS):
            rq = t * Q_ROWS + rl
            start = _win_start(rq)
            blks = []
            for m in range(lo // 2, hi // 2):
                v0 = start <= 2 * m < start + WIN_ROWS
                v1 = start <= 2 * m + 1 < start + WIN_ROWS
                if not (v0 or v1):
                    blks.append(neg)
                    continue
                i = 2 * m - rq + WIN_ROWS - 1
                assert 0 <= i < BIAS_ROWS
                mask = mask_both if (v0 and v1) else (mask_left if v0 else mask_right)
                c0 = (2 * m - lo) * GRID_W
                sb = s_w[rl * GRID_W:(rl + 1) * GRID_W, c0:c0 + LANES]
                blks.append(jnp.where(mask, sb * ATT_SCALE + tt_ref[i], NEG_INF))
            rows.append(jnp.concatenate(blks, axis=1))
        sw = jnp.concatenate(rows, axis=0)
        mx = jnp.maximum(jnp.max(sw, axis=-1, keepdims=True), jnp.max(s_c, axis=-1, keepdims=True))
        pw = jnp.exp(sw - mx)
        pc = jnp.exp(s_c - mx)
        denom = jnp.sum(pw, axis=-1, keepdims=True) + jnp.sum(pc, axis=-1, keepdims=True)
        o = jnp.dot(pw.astype(BF16), vb[lo * GRID_W:hi * GRID_W, :], preferred_element_type=F32)
        o = o + jnp.dot(pc.astype(BF16), cvb, preferred_element_type=F32)
        o_ref[t * tq:(t + 1) * tq, :] = (o / denom).astype(o_ref.dtype)


def _att_lat(l, z, cache_k, cache_v, tt, att):
    rb0 = T_CTX // DEC_SEQ
    hb = D_ATT // HEAD_DIM
    return pl.pallas_call(
        _att_lat_kernel,
        out_shape=jax.ShapeDtypeStruct((T_ALL, D_ATT), BF16),
        grid_spec=pltpu.PrefetchScalarGridSpec(
            num_scalar_prefetch=1,
            grid=(DEC_BATCH, N_HEADS),
            in_specs=[
                pl.BlockSpec((DEC_SEQ, HEAD_DIM), lambda b, h, l: (rb0 + b, h)),
                pl.BlockSpec((DEC_SEQ, HEAD_DIM), lambda b, h, l: (rb0 + b, hb + h)),
                pl.BlockSpec((DEC_SEQ, HEAD_DIM), lambda b, h, l: (rb0 + b, 2 * hb + h)),
                pl.BlockSpec((None, None, PAST_LEN * N_HEADS, HEAD_DIM), lambda b, h, l: (b, l[0], 0, 0)),
                pl.BlockSpec((None, None, PAST_LEN * N_HEADS, HEAD_DIM), lambda b, h, l: (b, l[0], 0, 0)),
                pl.BlockSpec((None, None, BIAS_ROWS, GRID_W, LANES), lambda b, h, l: (l[0], h, 0, 0, 0)),
                pl.BlockSpec(memory_space=pl.ANY),
            ],
            out_specs=pl.BlockSpec((DEC_SEQ, HEAD_DIM), lambda b, h, l: (rb0 + b, h)),
        ),
        input_output_aliases={7: 0},
        compiler_params=_cparams(2, 32),
        name="att_lat",
    )(l, z, z, z, cache_k, cache_v, tt, att)


CV_PAD = 16
CV_ROWS = 64


def _convmod_kernel(l_ref, a_ref, g_ref, w_ref, b_ref, lng_ref, lnb_ref, *rest):
    del l_ref
    o_ref, upad, ush = rest[-3:]
    s_len = a_ref.shape[0]
    n_sh = s_len + CV_PAD + SUBLANES
    upad[0:CV_PAD, :] = jnp.zeros((CV_PAD, D_CONV), F32)
    upad[CV_PAD + s_len:, :] = jnp.zeros((CV_PAD, D_CONV), F32)
    upad[CV_PAD:CV_PAD + s_len, :] = a_ref[...] * _sigmoid(g_ref[...])
    for j in range(SUBLANES):
        ush[j, 0:n_sh, :] = upad[j:j + n_sh, :]
    bias = b_ref[...]
    ln_g = lng_ref[...]
    ln_b = lnb_ref[...]

    def chunk(c, carry):
        r0 = pl.multiple_of(c * CV_ROWS, CV_ROWS)
        acc = jnp.broadcast_to(bias, (CV_ROWS, D_CONV))
        for k in range(CONV_WIDTH):
            m, j = divmod(k + CV_PAD - CONV_WIDTH // 2, SUBLANES)
            acc = acc + w_ref[k:k + 1, :] * ush[j, pl.ds(r0 + SUBLANES * m, CV_ROWS), :]
        mu = jnp.mean(acc, axis=-1, keepdims=True)
        xc = acc - mu
        var = jnp.mean(xc * xc, axis=-1, keepdims=True)
        y = xc * lax.rsqrt(var + EPS) * ln_g + ln_b
        o_ref[pl.ds(r0, CV_ROWS), :] = _silu(y).astype(o_ref.dtype)
        return carry

    lax.fori_loop(0, s_len // CV_ROWS, chunk, 0)


def _convmod(l, z, cv_w, cv_b, ln_g, ln_b, s_len, n_seq, row_block0, prev=None):
    cb = (3 * D_ATT) // D_CONV
    in_specs = [
        pl.BlockSpec((s_len, D_CONV), lambda b, l: (row_block0 + b, cb)),
        pl.BlockSpec((s_len, D_CONV), lambda b, l: (row_block0 + b, cb + 1)),
        pl.BlockSpec((None, CONV_WIDTH, D_CONV), lambda b, l: (l[0], 0, 0)),
        pl.BlockSpec((None, 1, D_CONV), lambda b, l: (l[0], 0, 0)),
        pl.BlockSpec((None, 1, D_CONV), lambda b, l: (l[0], 0, 0)),
        pl.BlockSpec((None, 1, D_CONV), lambda b, l: (l[0], 0, 0)),
    ]
    args = [l, z, z, cv_w, cv_b, ln_g, ln_b]
    aliases = {}
    if prev is not None:
        in_specs.append(pl.BlockSpec(memory_space=pl.ANY))
        args.append(prev)
        aliases = {len(args) - 1: 0}
    return pl.pallas_call(
        _convmod_kernel,
        out_shape=jax.ShapeDtypeStruct((T_ALL, D_CONV), BF16),
        grid_spec=pltpu.PrefetchScalarGridSpec(
            num_scalar_prefetch=1,
            grid=(n_seq,),
            in_specs=in_specs,
            out_specs=pl.BlockSpec((s_len, D_CONV), lambda b, l: (row_block0 + b, 0)),
            scratch_shapes=[
                pltpu.VMEM((s_len + 2 * CV_PAD, D_CONV), F32),
                pltpu.VMEM((SUBLANES, s_len + CV_PAD + SUBLANES, D_CONV), F32),
            ],
        ),
        input_output_aliases=aliases,
        compiler_params=_cparams(1, 48),
        name=f"convmod_{s_len}",
    )(*args)


LRU_PAD = 8
LRU_HALF = D_LRU // 2
LRU_CHUNK = 256


def _softplus(x):
    return jnp.maximum(x, 0.0) + jnp.log1p(jnp.exp(-jnp.abs(x)))


def _lru_kernel(l_ref, x_ref, g_ref, h0_ref, cw_ref, cb_ref, wa_ref, ba_ref, wi_ref, bi_ref, lam_ref, *rest):
    del l_ref
    y_ref, hl_ref, xpad, a_scr, b_scr, h_scr = rest[-6:]
    s_len = x_ref.shape[0]
    n_blk = s_len // SUBLANES
    n_lt = D_LRU // LANES

    xpad[0:LRU_PAD, :] = jnp.zeros((LRU_PAD, D_LRU), F32)
    xpad[LRU_PAD + s_len:, :] = jnp.zeros((LRU_PAD, D_LRU), F32)
    xpad[LRU_PAD:LRU_PAD + s_len, :] = x_ref[...]
    cb = cb_ref[...]
    decay = [-LRU_C * _softplus(-lam_ref[d:d + 1, :]) for d in range(2)]

    for c in range(s_len // LRU_CHUNK):
        r0 = c * LRU_CHUNK
        xc = jnp.broadcast_to(cb, (LRU_CHUNK, D_LRU))
        for k in range(LRU_CONV_WIDTH):
            off = LRU_PAD + r0 + k - LRU_CONV_WIDTH // 2
            xc = xc + cw_ref[k:k + 1, :] * xpad[off:off + LRU_CHUNK, :]
        xb = xc.astype(BF16)
        for d in range(2):
            ga = jnp.concatenate(
                [jnp.dot(xb[:, hh * LRU_HALF:(hh + 1) * LRU_HALF], wa_ref[d, hh], preferred_element_type=F32)
                 for hh in range(2)], axis=1) + ba_ref[d:d + 1, :]
            gi = jnp.concatenate(
                [jnp.dot(xb[:, hh * LRU_HALF:(hh + 1) * LRU_HALF], wi_ref[d, hh], preferred_element_type=F32)
                 for hh in range(2)], axis=1) + bi_ref[d:d + 1, :]
            log_a = _sigmoid(ga) * decay[d]
            th = jnp.tanh(log_a)
            gain = jnp.sqrt(-2.0 * th / (1.0 - th))
            a_scr[d, r0:r0 + LRU_CHUNK, :] = jnp.exp(log_a)
            b_scr[d, r0:r0 + LRU_CHUNK, :] = gain * (_sigmoid(gi) * xc)

    row = lax.broadcasted_iota(jnp.int32, (SUBLANES, LANES), 0)

    def scan_block(it, carry):
        new = []
        for d in range(2):
            blk = it if d == 0 else n_blk - 1 - it
            r0 = pl.multiple_of(blk * SUBLANES, SUBLANES)
            for c in range(n_lt):
                ls = slice(c * LANES, (c + 1) * LANES)
                a = a_scr[d, pl.ds(r0, SUBLANES), ls]
                b = b_scr[d, pl.ds(r0, SUBLANES), ls]
                for s in (1, 2, 4):
                    shift = s if d == 0 else SUBLANES - s
                    valid = (row >= s) if d == 0 else (row < SUBLANES - s)
                    a_sh = jnp.where(valid, pltpu.roll(a, shift, 0), 1.0)
                    b_sh = jnp.where(valid, pltpu.roll(b, shift, 0), 0.0)
                    b = a * b_sh + b
                    a = a * a_sh
                h = a * carry[d * n_lt + c] + b
                h_scr[d, pl.ds(r0, SUBLANES), ls] = h
                last = h[SUBLANES - 1:SUBLANES, :] if d == 0 else h[0:1, :]
                new.append(jnp.broadcast_to(last, (SUBLANES, LANES)))
        return tuple(new)

    init = tuple(
        jnp.broadcast_to(h0_ref[d:d + 1, c * LANES:(c + 1) * LANES], (SUBLANES, LANES))
        for d in range(2) for c in range(n_lt))
    final = lax.fori_loop(0, n_blk, scan_block, init)
    for d in range(2):
        for c in range(n_lt):
            hl_ref[d:d + 1, c * LANES:(c + 1) * LANES] = final[d * n_lt + c][0:1, :]

    for c in range(s_len // LRU_CHUNK):
        rs = slice(c * LRU_CHUNK, (c + 1) * LRU_CHUNK)
        y = (h_scr[0, rs, :] + h_scr[1, rs, :]) * _gelu_tanh(g_ref[rs, :])
        y_ref[rs, :] = y.astype(y_ref.dtype)


def _lru(l, z, h0, p, s_len, n_seq, row_block0, prev=None):
    cb = (3 * D_ATT + 2 * D_CONV) // D_LRU
    in_specs = [
        pl.BlockSpec((s_len, D_LRU), lambda b, l: (row_block0 + b, cb)),
        pl.BlockSpec((s_len, D_LRU), lambda b, l: (row_block0 + b, cb + 1)),
        pl.BlockSpec((None, 2, D_LRU), lambda b, l: (b, 0, 0)),
        pl.BlockSpec((None, LRU_CONV_WIDTH, D_LRU), lambda b, l: (l[0], 0, 0)),
        pl.BlockSpec((None, 1, D_LRU), lambda b, l: (l[0], 0, 0)),
        pl.BlockSpec((None, 2, 2, LRU_HALF, LRU_HALF), lambda b, l: (l[0], 0, 0, 0, 0)),
        pl.BlockSpec((None, 2, D_LRU), lambda b, l: (l[0], 0, 0)),
        pl.BlockSpec((None, 2, 2, LRU_HALF, LRU_HALF), lambda b, l: (l[0], 0, 0, 0, 0)),
        pl.BlockSpec((None, 2, D_LRU), lambda b, l: (l[0], 0, 0)),
        pl.BlockSpec((None, 2, D_LRU), lambda b, l: (l[0], 0, 0)),
    ]
    args = [l, z, z, h0, p["lru_conv_w"], p["lru_conv_b"], p["wa_dense"], p["lru_ba"], p["wi_dense"],
            p["lru_bi"], p["lru_lam"]]
    aliases = {}
    if prev is not None:
        in_specs.append(pl.BlockSpec(memory_space=pl.ANY))
        args.append(prev)
        aliases = {len(args) - 1: 0}
    return pl.pallas_call(
        _lru_kernel,
        out_shape=(jax.ShapeDtypeStruct((T_ALL, D_LRU), BF16),
                   jax.ShapeDtypeStruct((n_seq, 2, D_LRU), F32)),
        grid_spec=pltpu.PrefetchScalarGridSpec(
            num_scalar_prefetch=1,
            grid=(n_seq,),
            in_specs=in_specs,
            out_specs=(pl.BlockSpec((s_len, D_LRU), lambda b, l: (row_block0 + b, 0)),
                       pl.BlockSpec((None, 2, D_LRU), lambda b, l: (b, 0, 0))),
            scratch_shapes=[
                pltpu.VMEM((s_len + 2 * LRU_PAD, D_LRU), F32),
                pltpu.VMEM((2, s_len, D_LRU), F32),
                pltpu.VMEM((2, s_len, D_LRU), F32),
                pltpu.VMEM((2, s_len, D_LRU), F32),
            ],
        ),
        input_output_aliases=aliases,
        compiler_params=_cparams(1, 48),
        name=f"rglru_{s_len}",
    )(*args)


TM_OUT = 512


def _outproj_kernel(l_ref, att_ref, cv_ref, rec_ref, w_ref, x_ref, gate_ref, g_ref, sh_ref, sc_ref,
                    o_ref, h_ref):
    del l_ref
    y = jnp.dot(att_ref[...], w_ref[0:D_ATT, :], preferred_element_type=F32)
    y = y + jnp.dot(cv_ref[...], w_ref[D_ATT:D_ATT + D_CONV, :], preferred_element_type=F32)
    y = y + jnp.dot(rec_ref[...], w_ref[D_ATT + D_CONV:D_MIX, :], preferred_element_type=F32)
    o_ref[...] = x_ref[...] + gate_ref[...] * y
    gs = g_ref[...] * (1.0 + sc_ref[...])
    shift = sh_ref[...]
    for c in range(TM_OUT // NORM_ROWS):
        rs = slice(c * NORM_ROWS, (c + 1) * NORM_ROWS)
        xn = o_ref[rs, :]
        ms = jnp.mean(xn * xn, axis=-1, keepdims=True)
        h_ref[rs, :] = (xn * lax.rsqrt(ms + EPS) * gs + shift).astype(BF16)


def _outproj(l, att, cv, rec, w_out, x, ln_g, mod4):
    def mod_spec(chunk):
        return pl.BlockSpec((None, None, 1, D_MODEL),
                            lambda t, l: (l[0], _mod_row(t // (TM // TM_OUT)), 0, chunk))

    return pl.pallas_call(
        _outproj_kernel,
        out_shape=(jax.ShapeDtypeStruct((T_ALL, D_MODEL), F32),
                   jax.ShapeDtypeStruct((T_ALL, D_MODEL), BF16)),
        grid_spec=pltpu.PrefetchScalarGridSpec(
            num_scalar_prefetch=1,
            grid=(T_ALL // TM_OUT,),
            in_specs=[
                pl.BlockSpec((TM_OUT, D_ATT), lambda t, l: (t, 0)),
                pl.BlockSpec((TM_OUT, D_CONV), lambda t, l: (t, 0)),
                pl.BlockSpec((TM_OUT, D_LRU), lambda t, l: (t, 0)),
                pl.BlockSpec((None, D_MIX, D_MODEL), lambda t, l: (l[0], 0, 0)),
                pl.BlockSpec((TM_OUT, D_MODEL), lambda t, l: (t, 0)),
                mod_spec(2),
                pl.BlockSpec((None, 1, D_MODEL), lambda t, l: (l[0], 0, 0)),
                mod_spec(3),
                mod_spec(4),
            ],
            out_specs=(pl.BlockSpec((TM_OUT, D_MODEL), lambda t, l: (t, 0)),
                       pl.BlockSpec((TM_OUT, D_MODEL), lambda t, l: (t, 0))),
        ),
        compiler_params=_cparams(1, 48),
        name="outproj",
    )(l, att, cv, rec, w_out, x, mod4, ln_g, mod4, mod4)


TN_DOWN = 512
FFN_SPLIT = 2
FFN_HALO = 16
FFN_AHEAD = 3


def _ffn_kernel(l_ref, x_ref, h_scr, gate_ref, wug_ref, wuv_ref, cwg_ref, cwv_ref,
                cbg_ref, cbv_ref, wd_ref, o_ref, mprev_scr, mnext_scr):
    del l_ref
    i = pl.program_id(0)
    j = pl.program_id(1)

    @pl.when(j == 0)
    def _():
        o_ref[...] = jnp.zeros_like(o_ref)
        seq_len = jnp.where(i < CTX_MT, SEQ, DEC_SEQ)
        pos = lax.broadcasted_iota(jnp.int32, (TM, LANES), 0) & (seq_len - 1)
        mprev_scr[...] = jnp.where(pos == 0, 0.0, 1.0)
        mnext_scr[...] = jnp.where(pos == seq_len - 1, 0.0, 1.0)

    rows = TM // FFN_SPLIT
    def bounds(s):
        return max(s * rows - FFN_HALO, 0), min((s + 1) * rows + FFN_HALO, TM)

    def up(s):
        lo, hi = bounds(s)
        h = h_scr[lo:hi, :]
        return (_mm(h, wug_ref[...]), _mm(h, wuv_ref[...]))

    def conv(u, s, cw_ref, cb_ref):
        lo, hi = bounds(s)
        off = s * rows - lo
        prev = pltpu.roll(u, 1, 0) * jnp.tile(mprev_scr[lo:hi, :], (1, TF // LANES))
        nxt = pltpu.roll(u, hi - lo - 1, 0) * jnp.tile(mnext_scr[lo:hi, :], (1, TF // LANES))
        c = cw_ref[0:1, :] * prev + cw_ref[1:2, :] * u + cw_ref[2:3, :] * nxt + cb_ref[...]
        return c[off:off + rows, :]

    pending = [up(s) for s in range(min(FFN_AHEAD, FFN_SPLIT))]
    for s in range(FFN_SPLIT):
        if s + FFN_AHEAD < FFN_SPLIT:
            pending.append(up(s + FFN_AHEAD))
        u_cur = pending.pop(0)
        ug = conv(u_cur[0], s, cwg_ref, cbg_ref)
        uv = conv(u_cur[1], s, cwv_ref, cbv_ref)
        act = (_silu(ug) * uv).astype(BF16)
        rs = slice(s * rows, (s + 1) * rows)
        for n in range(D_MODEL // TN_DOWN):
            ns = slice(n * TN_DOWN, (n + 1) * TN_DOWN)
            o_ref[rs, ns] = _mm(act, wd_ref[:, ns]) + o_ref[rs, ns]

    @pl.when(j == N_FT - 1)
    def _():
        o_ref[...] = x_ref[...] + gate_ref[...] * o_ref[...]


def _ffn(l, x, h2, mod4, ffn_up, conv_w, conv_b, ffn_down):
    return pl.pallas_call(
        _ffn_kernel,
        out_shape=jax.ShapeDtypeStruct((T_ALL, D_MODEL), F32),
        grid_spec=pltpu.PrefetchScalarGridSpec(
            num_scalar_prefetch=1,
            grid=(N_MT, N_FT),
            in_specs=[
                pl.BlockSpec((TM, D_MODEL), lambda i, j, l: (i, 0), pipeline_mode=pl.Buffered(1)),
                pl.BlockSpec((TM, D_MODEL), lambda i, j, l: (i, 0), pipeline_mode=pl.Buffered(1)),
                _mod_spec(5),
                pl.BlockSpec((None, D_MODEL, TF), lambda i, j, l: (l[0], 0, j)),
                pl.BlockSpec((None, D_MODEL, TF), lambda i, j, l: (l[0], 0, N_FT + j)),
                pl.BlockSpec((None, 3, TF), lambda i, j, l: (l[0], 0, j)),
                pl.BlockSpec((None, 3, TF), lambda i, j, l: (l[0], 0, N_FT + j)),
                pl.BlockSpec((None, 1, TF), lambda i, j, l: (l[0], 0, j)),
                pl.BlockSpec((None, 1, TF), lambda i, j, l: (l[0], 0, N_FT + j)),
                pl.BlockSpec((None, TF, D_MODEL), lambda i, j, l: (l[0], j, 0)),
            ],
            out_specs=pl.BlockSpec((TM, D_MODEL), lambda i, j, l: (i, 0)),
            scratch_shapes=[
                pltpu.VMEM((TM, LANES), F32),
                pltpu.VMEM((TM, LANES), F32),
            ],
        ),
        compiler_params=_cparams(2, 62),
        name="convffn",
    )(l, x, h2, mod4, ffn_up, ffn_up, conv_w, conv_w, conv_b, conv_b, ffn_down)


def _final_kernel(x_ref, g_ref, o_ref):
    g = g_ref[...]
    for c in range(TM_FINAL // NORM_ROWS):
        rs = slice(c * NORM_ROWS, (c + 1) * NORM_ROWS)
        x = x_ref[rs, :]
        ms = jnp.mean(x * x, axis=-1, keepdims=True)
        o_ref[rs, :] = x * lax.rsqrt(ms + EPS) * g


def _final_norm(x, g, row0, n_rows):
    tile0 = row0 // TM_FINAL
    return pl.pallas_call(
        _final_kernel,
        out_shape=jax.ShapeDtypeStruct((n_rows, D_MODEL), F32),
        grid=(n_rows // TM_FINAL,),
        in_specs=[
            pl.BlockSpec((TM_FINAL, D_MODEL), lambda i: (tile0 + i, 0)),
            pl.BlockSpec((1, D_MODEL), lambda i: (0, 0)),
        ],
        out_specs=pl.BlockSpec((TM_FINAL, D_MODEL), lambda i: (i, 0)),
        compiler_params=_cparams(1, 32),
        name="final_norm",
    )(x, g)


def _block_diag_halves(w):
    nb = LRU_BLOCKS // 2
    w6 = w.reshape(DEPTH, 2, 2, nb, LRU_BW, LRU_BW)
    eye = jnp.eye(nb, dtype=w.dtype)
    dense = jnp.einsum("ldhjab,jk->ldhjakb", w6, eye)
    dense = dense.reshape(DEPTH, 2, 2, LRU_HALF, LRU_HALF).astype(BF16)
    return lax.optimization_barrier(dense)


def kernel(x_prompt, x_sample, cache_k, cache_v, state_lru, c, c_ctx, ln1_g, w_mod, b_mod, w_in, na_bias,
           cv_w, cv_b, cv_ln_g, cv_ln_b, lru_conv_w, lru_conv_b, lru_wa, lru_ba, lru_wi, lru_bi, lru_lam,
           w_out, ln2_g, ffn_up, ffn_conv_w, ffn_conv_b, ffn_down, final_g):
    x = jnp.concatenate([x_prompt.reshape(T_CTX, D_MODEL), x_sample.reshape(T_LAT, D_MODEL)], axis=0)
    cond = jnp.concatenate([c_ctx[None, :], c, jnp.zeros((N_COND - 1 - DEC_BATCH, D_MODEL), F32)], axis=0)
    mod4 = _modulation(cond, w_mod, b_mod).reshape(DEPTH, N_COND, 1, 6 * D_MODEL)
    tt = _bias_table(na_bias)
    ck = cache_k.reshape(DEC_BATCH, DEPTH, PAST_LEN * N_HEADS, HEAD_DIM)
    cvv = cache_v.reshape(DEC_BATCH, DEPTH, PAST_LEN * N_HEADS, HEAD_DIM)
    state = jnp.swapaxes(state_lru, 0, 1)
    h0_ctx = jnp.zeros((BATCH, 2, D_LRU), F32)
    lru_p = {
        "lru_conv_w": lru_conv_w,
        "lru_conv_b": lru_conv_b.reshape(DEPTH, 1, D_LRU),
        "wa_dense": _block_diag_halves(lru_wa),
        "lru_ba": lru_ba,
        "wi_dense": _block_diag_halves(lru_wi),
        "lru_bi": lru_bi,
        "lru_lam": lru_lam,
    }
    ln1 = ln1_g.reshape(DEPTH, 1, D_MODEL)
    ln2 = ln2_g.reshape(DEPTH, 1, D_MODEL)
    cvb = cv_b.reshape(DEPTH, 1, D_CONV)
    cvg = cv_ln_g.reshape(DEPTH, 1, D_CONV)
    cvlb = cv_ln_b.reshape(DEPTH, 1, D_CONV)
    fcb = ffn_conv_b.reshape(DEPTH, 1, 2 * D_FF)
    lat_sb = T_CTX // DEC_SEQ
    w_out = w_out.astype(BF16)

    def layer(carry, xs):
        x, ks, vs, hs = carry
        li, h0_lat = xs
        l = jnp.reshape(li, (1,))
        z = _inproj(l, x, ln1, mod4, w_in)
        att, ks, vs = _att_ctx(l, z, ks, vs)
        att = _att_lat(l, z, ck, cvv, tt, att)
        cvo = _convmod(l, z, cv_w, cvb, cvg, cvlb, SEQ, BATCH, 0)
        cvo = _convmod(l, z, cv_w, cvb, cvg, cvlb, DEC_SEQ, DEC_BATCH, lat_sb, prev=cvo)
        rec, h_last = _lru(l, z, h0_ctx, lru_p, SEQ, BATCH, 0)
        rec, _ = _lru(l, z, h0_lat, lru_p, DEC_SEQ, DEC_BATCH, lat_sb, prev=rec)
        x, h2 = _outproj(l, att, cvo, rec, w_out, x, ln2, mod4)
        x = _ffn(l, x, h2, mod4, ffn_up, ffn_conv_w, fcb, ffn_down)
        hs = lax.dynamic_update_slice(hs, h_last[:, None], (0, li, 0, 0))
        return (x, ks, vs, hs), None

    init = (x,
            jnp.zeros((BATCH, DEPTH, SEQ * N_HEADS, HEAD_DIM), F32),
            jnp.zeros((BATCH, DEPTH, SEQ * N_HEADS, HEAD_DIM), F32),
            jnp.zeros((BATCH, DEPTH, 2, D_LRU), F32))
    (x, ks, vs, hs), _ = lax.scan(layer, init, (jnp.arange(DEPTH, dtype=jnp.int32), state))

    g, x = lax.optimization_barrier((final_g.reshape(1, D_MODEL), x))
    y_prompt = _final_norm(x, g, 0, T_CTX).reshape(BATCH, SEQ, D_MODEL)
    y_sample = _final_norm(x, g, T_CTX, T_LAT).reshape(DEC_BATCH, DEC_SEQ, D_MODEL)
    new_k = ks.reshape(BATCH, DEPTH, SEQ, N_HEADS, HEAD_DIM)
    new_v = vs.reshape(BATCH, DEPTH, SEQ, N_HEADS, HEAD_DIM)
    return (y_prompt, y_sample, new_k, new_v, hs)
```

```python
import functools

import jax
import jax.numpy as jnp
from jax import lax
from jax.experimental import pallas as pl
from jax.experimental.pallas import tpu as pltpu

F32 = jnp.float32
BF16 = jnp.bfloat16

D_MODEL = 2048
BATCH = 16
SEQ = 256
DEPTH = 4
DEC_BATCH = 2
DEC_SEQ = 1024
PAST_LEN = 512
GRID_W = 64
GRID_R = DEC_SEQ // GRID_W
D_ATT = 1024
N_HEADS = 8
HEAD_DIM = 128
D_CONV = 512
D_LRU = 512
D_MIX = D_ATT + D_CONV + D_LRU
D_IN = 3 * D_ATT + 2 * D_CONV + 2 * D_LRU
WIN_ROWS = 8
WIN_COLS = 16
CONV_WIDTH = 31
LRU_CONV_WIDTH = 4
LRU_BLOCKS = 8
LRU_BW = D_LRU // LRU_BLOCKS
LRU_C = 8.0
D_FF = 5632
EPS = 1e-6
ATT_SCALE = HEAD_DIM ** -0.5
NEG_INF = -1e30

T_CTX = BATCH * SEQ
T_LAT = DEC_BATCH * DEC_SEQ
T_ALL = T_CTX + T_LAT

V7X_VMEM_BYTES = 64 * 1024 * 1024
SUBLANES = 8
LANES = 128

TM = 1024
N_MT = T_ALL // TM
CTX_MT = T_CTX // TM
TM2 = 2 * TM
N_MT2 = T_ALL // TM2
TM_FINAL = 256
TN_IN = 512
TF = 512
N_FT = D_FF // TF
TN_MOD = 1024
NORM_ROWS = 128
N_COND = 8
BIAS_ROWS = 2 * WIN_ROWS - 1
BIAS_COLS = 2 * WIN_COLS - 1


def _cparams(n_axes, vmem_mib, flags=None):
    return pltpu.CompilerParams(
        dimension_semantics=("arbitrary",) * n_axes,
        vmem_limit_bytes=min(vmem_mib * 1024 * 1024, V7X_VMEM_BYTES - 2 * 1024 * 1024),
        flags=flags,
    )


def _sigmoid(x):
    return 0.5 * jnp.tanh(0.5 * x) + 0.5


def _silu(x):
    return x * _sigmoid(x)


def _mm(a, b):
    return lax.dot_general(a, b, (((1,), (0,)), ((), ())), preferred_element_type=F32)


def _gelu_tanh(x):
    return 0.5 * x * (1.0 + jnp.tanh(0.7978845608028654 * (x + 0.044715 * (x * x * x))))


def _mod_row(i):
    return jnp.maximum(i - (CTX_MT - 1), 0)


def _mod_kernel(c_ref, w_ref, b_ref, o_ref):
    a = _silu(c_ref[...]).astype(BF16)
    o_ref[...] = jnp.dot(a, w_ref[...].astype(BF16), preferred_element_type=F32) + b_ref[...]


def _modulation(cond, w_mod, b_mod):
    n_out = w_mod.shape[-1]
    return pl.pallas_call(
        _mod_kernel,
        out_shape=jax.ShapeDtypeStruct((DEPTH, N_COND, n_out), F32),
        grid=(DEPTH, n_out // TN_MOD),
        in_specs=[
            pl.BlockSpec((N_COND, D_MODEL), lambda l, j: (0, 0)),
            pl.BlockSpec((None, D_MODEL, TN_MOD), lambda l, j: (l, 0, j)),
            pl.BlockSpec((None, 1, TN_MOD), lambda l, j: (l, 0, j)),
        ],
        out_specs=pl.BlockSpec((None, N_COND, TN_MOD), lambda l, j: (l, 0, j)),
        compiler_params=_cparams(2, 40),
        name="modulation",
    )(cond, w_mod, b_mod.reshape(DEPTH, 1, n_out))


def _bias_kernel(rb_ref, o_ref, td_scr):
    base = (pl.program_id(0) * N_HEADS + pl.program_id(1)) * (BIAS_ROWS * BIAS_COLS)
    qi = lax.broadcasted_iota(jnp.int32, (GRID_W, LANES), 0)
    li = lax.broadcasted_iota(jnp.int32, (GRID_W, LANES), 1)
    diff = jnp.clip((li & (GRID_W - 1)) - qi, -(WIN_COLS - 1), WIN_COLS - 1) + (WIN_COLS - 1)

    def body(i, carry):
        acc = jnp.zeros((GRID_W, LANES), F32)
        for d in range(BIAS_COLS):
            acc = jnp.where(diff == d, rb_ref[base + i * BIAS_COLS + d], acc)
        td_scr[i] = acc
        return carry

    lax.fori_loop(0, BIAS_ROWS, body, 0)
    td_scr[BIAS_ROWS] = jnp.zeros((GRID_W, LANES), F32)
    left = li < GRID_W
    for i in range(BIAS_ROWS):
        o_ref[i] = jnp.where(left, td_scr[i], td_scr[i + 1])


def _bias_table(na_bias):
    return pl.pallas_call(
        _bias_kernel,
        out_shape=jax.ShapeDtypeStruct((DEPTH, N_HEADS, BIAS_ROWS, GRID_W, LANES), F32),
        grid=(DEPTH, N_HEADS),
        in_specs=[pl.BlockSpec(memory_space=pltpu.SMEM)],
        out_specs=pl.BlockSpec((None, None, BIAS_ROWS, GRID_W, LANES), lambda l, h: (l, h, 0, 0, 0)),
        scratch_shapes=[pltpu.VMEM((BIAS_ROWS + 1, GRID_W, LANES), F32)],
        compiler_params=_cparams(2, 16),
        name="bias_table",
    )(na_bias.reshape(-1))


def _norm_mod(x_ref, g_ref, sc_ref, sh_ref, h_scr, row0=0):
    gs = g_ref[...] * (1.0 + sc_ref[...])
    shift = sh_ref[...]

    def body(c, carry):
        r0 = pl.multiple_of(row0 + c * NORM_ROWS, NORM_ROWS)
        x = x_ref[pl.ds(r0, NORM_ROWS), :]
        ms = jnp.mean(x * x, axis=-1, keepdims=True)
        h_scr[pl.ds(r0, NORM_ROWS), :] = (x * lax.rsqrt(ms + EPS) * gs + shift).astype(BF16)
        return carry

    lax.fori_loop(0, TM // NORM_ROWS, body, 0)


def _mod_spec(chunk):
    return pl.BlockSpec((None, None, 1, D_MODEL), lambda i, j, l: (l[0], _mod_row(i), 0, chunk))


def _mod_spec2(chunk, half):
    return pl.BlockSpec((None, None, 1, D_MODEL), lambda i, j, l: (l[0], _mod_row(2 * i + half), 0, chunk))


def _inproj_kernel(l_ref, x_ref, g_ref, sh0_ref, sc0_ref, sh1_ref, sc1_ref, w_ref, o_ref, h_scr):
    @pl.when(pl.program_id(1) == 0)
    def _():
        _norm_mod(x_ref, g_ref, sc0_ref, sh0_ref, h_scr, 0)
        _norm_mod(x_ref, g_ref, sc1_ref, sh1_ref, h_scr, TM)

    o_ref[...] = _mm(h_scr[...], w_ref[...])


def _inproj(l, x, ln_g, mod4, w_in):
    return pl.pallas_call(
        _inproj_kernel,
        out_shape=jax.ShapeDtypeStruct((T_ALL, D_IN), F32),
        grid_spec=pltpu.PrefetchScalarGridSpec(
            num_scalar_prefetch=1,
            grid=(N_MT2, D_IN // TN_IN),
            in_specs=[
                pl.BlockSpec((TM2, D_MODEL), lambda i, j, l: (i, 0)),
                pl.BlockSpec((None, 1, D_MODEL), lambda i, j, l: (l[0], 0, 0)),
                _mod_spec2(0, 0),
                _mod_spec2(1, 0),
                _mod_spec2(0, 1),
                _mod_spec2(1, 1),
                pl.BlockSpec((None, D_MODEL, TN_IN), lambda i, j, l: (l[0], 0, j)),
            ],
            out_specs=pl.BlockSpec((TM2, TN_IN), lambda i, j, l: (i, j)),
            scratch_shapes=[pltpu.VMEM((TM2, D_MODEL), BF16)],
        ),
        compiler_params=_cparams(2, 62),
        name="inproj",
    )(l, x, ln_g, mod4, mod4, mod4, mod4, w_in)


def _dot_nt(a, b):
    return lax.dot_general(a, b, (((1,), (1,)), ((), ())), preferred_element_type=F32)


def _att_ctx_kernel(l_ref, q_ref, k_ref, v_ref, ks_in_ref, vs_in_ref, o_ref, nk_ref, nv_ref):
    del l_ref, ks_in_ref, vs_in_ref
    for h in range(N_HEADS):
        sl = slice(h * HEAD_DIM, (h + 1) * HEAD_DIM)
        kf = k_ref[:, sl]
        vf = v_ref[:, sl]
        nk_ref[pl.ds(h, SEQ, stride=N_HEADS), :] = kf
        nv_ref[pl.ds(h, SEQ, stride=N_HEADS), :] = vf
        q = q_ref[:, sl].astype(BF16)
        k = kf.astype(BF16)
        v = vf.astype(BF16)
        s = _dot_nt(q, k) * ATT_SCALE
        p = jnp.exp(s - jnp.max(s, axis=-1, keepdims=True))
        denom = jnp.sum(p, axis=-1, keepdims=True)
        o = jnp.dot(p.astype(BF16), v, preferred_element_type=F32)
        o_ref[:, sl] = (o / denom).astype(o_ref.dtype)


def _att_ctx(l, z, ks, vs):
    kv_shape = jax.ShapeDtypeStruct((BATCH, DEPTH, SEQ * N_HEADS, HEAD_DIM), F32)
    kv_spec = pl.BlockSpec((None, None, SEQ * N_HEADS, HEAD_DIM), lambda b, l: (b, l[0], 0, 0))
    return pl.pallas_call(
        _att_ctx_kernel,
        out_shape=(jax.ShapeDtypeStruct((T_ALL, D_ATT), BF16), kv_shape, kv_shape),
        grid_spec=pltpu.PrefetchScalarGridSpec(
            num_scalar_prefetch=1,
            grid=(BATCH,),
            in_specs=[
                pl.BlockSpec((SEQ, D_ATT), lambda b, l: (b, 0)),
                pl.BlockSpec((SEQ, D_ATT), lambda b, l: (b, 1)),
                pl.BlockSpec((SEQ, D_ATT), lambda b, l: (b, 2)),
                pl.BlockSpec(memory_space=pl.ANY),
                pl.BlockSpec(memory_space=pl.ANY),
            ],
            out_specs=(pl.BlockSpec((SEQ, D_ATT), lambda b, l: (b, 0)), kv_spec, kv_spec),
        ),
        input_output_aliases={4: 1, 5: 2},
        compiler_params=_cparams(1, 32),
        name="att_ctx",
    )(l, z, z, z, ks, vs)


Q_ROWS = 4
_KEY_ROWS = ((0, 8), (0, 12), (4, 16), (8, 16))


def _win_start(rq):
    return min(max(rq - WIN_ROWS // 2, 0), GRID_R - WIN_ROWS)


def _att_lat_kernel(l_ref, q_ref, k_ref, v_ref, ck_ref, cv_ref, tt_ref, att_in_ref, o_ref):
    del l_ref, att_in_ref
    qi = lax.broadcasted_iota(jnp.int32, (GRID_W, LANES), 0)
    li = lax.broadcasted_iota(jnp.int32, (GRID_W, LANES), 1)
    kc = li & (GRID_W - 1)
    cs = jnp.clip(qi - WIN_COLS // 2, 0, GRID_W - WIN_COLS)
    in_cols = (kc >= cs) & (kc < cs + WIN_COLS)
    left = li < GRID_W
    mask_both = in_cols
    mask_left = in_cols & left
    mask_right = in_cols & jnp.logical_not(left)
    neg = jnp.full((GRID_W, LANES), NEG_INF, F32)

    kb = k_ref[...].astype(BF16)
    vb = v_ref[...].astype(BF16)
    head = pl.program_id(1)
    ckb = ck_ref[pl.ds(head, PAST_LEN, stride=N_HEADS), :].astype(BF16)
    cvb = cv_ref[pl.ds(head, PAST_LEN, stride=N_HEADS), :].astype(BF16)
    tq = Q_ROWS * GRID_W
    n_tiles = GRID_R // Q_ROWS

    def scores(t):
        lo, hi = _KEY_ROWS[t]
        q = q_ref[t * tq:(t + 1) * tq, :].astype(BF16)
        return _dot_nt(q, kb[lo * GRID_W:hi * GRID_W, :]), _dot_nt(q, ckb)

    nxt = scores(0)
    for t in range(n_tiles):
        lo, hi = _KEY_ROWS[t]
        s_w, s_c = nxt
        if t + 1 < n_tiles:
            nxt = scores(t + 1)
        s_c = s_c * ATT_SCALE
        rows = []
        for rl in range(Q_ROWS):
            rq = t * Q_ROWS + rl
            start = _win_start(rq)
            blks = []
            for m in range(lo // 2, hi // 2):
                v0 = start <= 2 * m < start + WIN_ROWS
                v1 = start <= 2 * m + 1 < start + WIN_ROWS
                if not (v0 or v1):
                    blks.append(neg)
                    continue
                i = 2 * m - rq + WIN_ROWS - 1
                assert 0 <= i < BIAS_ROWS
                mask = mask_both if (v0 and v1) else (mask_left if v0 else mask_right)
                c0 = (2 * m - lo) * GRID_W
                sb = s_w[rl * GRID_W:(rl + 1) * GRID_W, c0:c0 + LANES]
                blks.append(jnp.where(mask, sb * ATT_SCALE + tt_ref[i], NEG_INF))
            rows.append(jnp.concatenate(blks, axis=1))
        sw = jnp.concatenate(rows, axis=0)
        mx = jnp.maximum(jnp.max(sw, axis=-1, keepdims=True), jnp.max(s_c, axis=-1, keepdims=True))
        pw = jnp.exp(sw - mx)
        pc = jnp.exp(s_c - mx)
        denom = jnp.sum(pw, axis=-1, keepdims=True) + jnp.sum(pc, axis=-1, keepdims=True)
        o = jnp.dot(pw.astype(BF16), vb[lo * GRID_W:hi * GRID_W, :], preferred_element_type=F32)
        o = o + jnp.dot(pc.astype(BF16), cvb, preferred_element_type=F32)
        o_ref[t * tq:(t + 1) * tq, :] = (o / denom).astype(o_ref.dtype)


def _att_lat(l, z, cache_k, cache_v, tt, att):
    rb0 = T_CTX // DEC_SEQ
    hb = D_ATT // HEAD_DIM
    return pl.pallas_call(
        _att_lat_kernel,
        out_shape=jax.ShapeDtypeStruct((T_ALL, D_ATT), BF16),
        grid_spec=pltpu.PrefetchScalarGridSpec(
            num_scalar_prefetch=1,
            grid=(DEC_BATCH, N_HEADS),
            in_specs=[
                pl.BlockSpec((DEC_SEQ, HEAD_DIM), lambda b, h, l: (rb0 + b, h)),
                pl.BlockSpec((DEC_SEQ, HEAD_DIM), lambda b, h, l: (rb0 + b, hb + h)),
                pl.BlockSpec((DEC_SEQ, HEAD_DIM), lambda b, h, l: (rb0 + b, 2 * hb + h)),
                pl.BlockSpec((None, None, PAST_LEN * N_HEADS, HEAD_DIM), lambda b, h, l: (b, l[0], 0, 0)),
                pl.BlockSpec((None, None, PAST_LEN * N_HEADS, HEAD_DIM), lambda b, h, l: (b, l[0], 0, 0)),
                pl.BlockSpec((None, None, BIAS_ROWS, GRID_W, LANES), lambda b, h, l: (l[0], h, 0, 0, 0)),
                pl.BlockSpec(memory_space=pl.ANY),
            ],
            out_specs=pl.BlockSpec((DEC_SEQ, HEAD_DIM), lambda b, h, l: (rb0 + b, h)),
        ),
        input_output_aliases={7: 0},
        compiler_params=_cparams(2, 32),
        name="att_lat",
    )(l, z, z, z, cache_k, cache_v, tt, att)


CV_PAD = 16
CV_ROWS = 64


def _convmod_kernel(l_ref, a_ref, g_ref, w_ref, vec_ref, *rest):
    del l_ref
    o_ref, upad, ush = rest[-3:]
    s_len = a_ref.shape[0]
    n_sh = s_len + CV_PAD + SUBLANES
    upad[0:CV_PAD, :] = jnp.zeros((CV_PAD, D_CONV), F32)
    upad[CV_PAD + s_len:, :] = jnp.zeros((CV_PAD, D_CONV), F32)
    upad[CV_PAD:CV_PAD + s_len, :] = a_ref[...] * _sigmoid(g_ref[...])
    for j in range(SUBLANES):
        ush[j, 0:n_sh, :] = upad[j:j + n_sh, :]
    bias = vec_ref[0:1, :]
    ln_g = vec_ref[1:2, :]
    ln_b = vec_ref[2:3, :]

    def chunk(c, carry):
        r0 = pl.multiple_of(c * CV_ROWS, CV_ROWS)
        acc = jnp.broadcast_to(bias, (CV_ROWS, D_CONV))
        for k in range(CONV_WIDTH):
            m, j = divmod(k + CV_PAD - CONV_WIDTH // 2, SUBLANES)
            acc = acc + w_ref[k:k + 1, :] * ush[j, pl.ds(r0 + SUBLANES * m, CV_ROWS), :]
        mu = jnp.mean(acc, axis=-1, keepdims=True)
        xc = acc - mu
        var = jnp.mean(xc * xc, axis=-1, keepdims=True)
        y = xc * lax.rsqrt(var + EPS) * ln_g + ln_b
        o_ref[pl.ds(r0, CV_ROWS), :] = _silu(y).astype(o_ref.dtype)
        return carry

    lax.fori_loop(0, s_len // CV_ROWS, chunk, 0)


def _convmod(l, z, cv_w, cv_vec, s_len, n_seq, row_block0, prev=None):
    cb = (3 * D_ATT) // D_CONV
    in_specs = [
        pl.BlockSpec((s_len, D_CONV), lambda b, l: (row_block0 + b, cb)),
        pl.BlockSpec((s_len, D_CONV), lambda b, l: (row_block0 + b, cb + 1)),
        pl.BlockSpec((None, CONV_WIDTH, D_CONV), lambda b, l: (l[0], 0, 0)),
        pl.BlockSpec((None, SUBLANES, D_CONV), lambda b, l: (l[0], 0, 0)),
    ]
    args = [l, z, z, cv_w, cv_vec]
    aliases = {}
    if prev is not None:
        in_specs.append(pl.BlockSpec(memory_space=pl.ANY))
        args.append(prev)
        aliases = {len(args) - 1: 0}
    return pl.pallas_call(
        _convmod_kernel,
        out_shape=jax.ShapeDtypeStruct((T_ALL, D_CONV), BF16),
        grid_spec=pltpu.PrefetchScalarGridSpec(
            num_scalar_prefetch=1,
            grid=(n_seq,),
            in_specs=in_specs,
            out_specs=pl.BlockSpec((s_len, D_CONV), lambda b, l: (row_block0 + b, 0)),
            scratch_shapes=[
                pltpu.VMEM((s_len + 2 * CV_PAD, D_CONV), F32),
                pltpu.VMEM((SUBLANES, s_len + CV_PAD + SUBLANES, D_CONV), F32),
            ],
        ),
        input_output_aliases=aliases,
        compiler_params=_cparams(1, 48),
        name=f"convmod_{s_len}",
    )(*args)


LRU_PAD = 8
LRU_HALF = D_LRU // 2
LRU_CHUNK = 256


def _softplus(x):
    return jnp.maximum(x, 0.0) + jnp.log1p(jnp.exp(-jnp.abs(x)))


def _lru_kernel(l_ref, x_ref, g_ref, h0_ref, cw_ref, wa_ref, wi_ref, vec_ref, *rest):
    del l_ref
    cb_ref = vec_ref.at[0:1]
    ba_ref = vec_ref.at[1:3]
    bi_ref = vec_ref.at[3:5]
    lam_ref = vec_ref.at[5:7]
    y_ref, hl_ref, xpad, a_scr, b_scr, h_scr = rest[-6:]
    s_len = x_ref.shape[0]
    n_blk = s_len // SUBLANES
    n_lt = D_LRU // LANES

    xpad[0:LRU_PAD, :] = jnp.zeros((LRU_PAD, D_LRU), F32)
    xpad[LRU_PAD + s_len:, :] = jnp.zeros((LRU_PAD, D_LRU), F32)
    xpad[LRU_PAD:LRU_PAD + s_len, :] = x_ref[...]
    cb = cb_ref[...]
    decay = [-LRU_C * _softplus(-lam_ref[d:d + 1, :]) for d in range(2)]

    for c in range(s_len // LRU_CHUNK):
        r0 = c * LRU_CHUNK
        xc = jnp.broadcast_to(cb, (LRU_CHUNK, D_LRU))
        for k in range(LRU_CONV_WIDTH):
            off = LRU_PAD + r0 + k - LRU_CONV_WIDTH // 2
            xc = xc + cw_ref[k:k + 1, :] * xpad[off:off + LRU_CHUNK, :]
        xb = xc.astype(BF16)
        for d in range(2):
            ga = jnp.concatenate(
                [jnp.dot(xb[:, hh * LRU_HALF:(hh + 1) * LRU_HALF], wa_ref[d, hh], preferred_element_type=F32)
                 for hh in range(2)], axis=1) + ba_ref[d:d + 1, :]
            gi = jnp.concatenate(
                [jnp.dot(xb[:, hh * LRU_HALF:(hh + 1) * LRU_HALF], wi_ref[d, hh], preferred_element_type=F32)
                 for hh in range(2)], axis=1) + bi_ref[d:d + 1, :]
            log_a = _sigmoid(ga) * decay[d]
            th = jnp.tanh(log_a)
            gain = jnp.sqrt(-2.0 * th / (1.0 - th))
            a_scr[d, r0:r0 + LRU_CHUNK, :] = jnp.exp(log_a)
            b_scr[d, r0:r0 + LRU_CHUNK, :] = gain * (_sigmoid(gi) * xc)

    row = lax.broadcasted_iota(jnp.int32, (SUBLANES, LANES), 0)

    def scan_block(it, carry):
        new = []
        for d in range(2):
            blk = it if d == 0 else n_blk - 1 - it
            r0 = pl.multiple_of(blk * SUBLANES, SUBLANES)
            for c in range(n_lt):
                ls = slice(c * LANES, (c + 1) * LANES)
                a = a_scr[d, pl.ds(r0, SUBLANES), ls]
                b = b_scr[d, pl.ds(r0, SUBLANES), ls]
                for s in (1, 2, 4):
                    shift = s if d == 0 else SUBLANES - s
                    valid = (row >= s) if d == 0 else (row < SUBLANES - s)
                    a_sh = jnp.where(valid, pltpu.roll(a, shift, 0), 1.0)
                    b_sh = jnp.where(valid, pltpu.roll(b, shift, 0), 0.0)
                    b = a * b_sh + b
                    a = a * a_sh
                h = a * carry[d * n_lt + c] + b
                h_scr[d, pl.ds(r0, SUBLANES), ls] = h
                last = h[SUBLANES - 1:SUBLANES, :] if d == 0 else h[0:1, :]
                new.append(jnp.broadcast_to(last, (SUBLANES, LANES)))
        return tuple(new)

    init = tuple(
        jnp.broadcast_to(h0_ref[d:d + 1, c * LANES:(c + 1) * LANES], (SUBLANES, LANES))
        for d in range(2) for c in range(n_lt))
    final = lax.fori_loop(0, n_blk, scan_block, init)
    hl_ref[...] = jnp.zeros(hl_ref.shape, F32)
    for d in range(2):
        for c in range(n_lt):
            hl_ref[d:d + 1, c * LANES:(c + 1) * LANES] = final[d * n_lt + c][0:1, :]

    for c in range(s_len // LRU_CHUNK):
        rs = slice(c * LRU_CHUNK, (c + 1) * LRU_CHUNK)
        y = (h_scr[0, rs, :] + h_scr[1, rs, :]) * _gelu_tanh(g_ref[rs, :])
        y_ref[rs, :] = y.astype(y_ref.dtype)


def _lru(l, z, h0, p, s_len, n_seq, row_block0, prev=None):
    cb = (3 * D_ATT + 2 * D_CONV) // D_LRU
    in_specs = [
        pl.BlockSpec((s_len, D_LRU), lambda b, l: (row_block0 + b, cb)),
        pl.BlockSpec((s_len, D_LRU), lambda b, l: (row_block0 + b, cb + 1)),
        pl.BlockSpec((None, SUBLANES, D_LRU), lambda b, l: (b, 0, 0)),
        pl.BlockSpec((None, LRU_CONV_WIDTH, D_LRU), lambda b, l: (l[0], 0, 0)),
        pl.BlockSpec((None, 2, 2, LRU_HALF, LRU_HALF), lambda b, l: (l[0], 0, 0, 0, 0)),
        pl.BlockSpec((None, 2, 2, LRU_HALF, LRU_HALF), lambda b, l: (l[0], 0, 0, 0, 0)),
        pl.BlockSpec((None, SUBLANES, D_LRU), lambda b, l: (l[0], 0, 0)),
    ]
    args = [l, z, z, h0, p["lru_conv_w"], p["wa_dense"], p["wi_dense"], p["vec"]]
    aliases = {}
    if prev is not None:
        in_specs.append(pl.BlockSpec(memory_space=pl.ANY))
        args.append(prev)
        aliases = {len(args) - 1: 0}
    return pl.pallas_call(
        _lru_kernel,
        out_shape=(jax.ShapeDtypeStruct((T_ALL, D_LRU), BF16),
                   jax.ShapeDtypeStruct((n_seq, SUBLANES, D_LRU), F32)),
        grid_spec=pltpu.PrefetchScalarGridSpec(
            num_scalar_prefetch=1,
            grid=(n_seq,),
            in_specs=in_specs,
            out_specs=(pl.BlockSpec((s_len, D_LRU), lambda b, l: (row_block0 + b, 0)),
                       pl.BlockSpec((None, SUBLANES, D_LRU), lambda b, l: (b, 0, 0))),
            scratch_shapes=[
                pltpu.VMEM((s_len + 2 * LRU_PAD, D_LRU), F32),
                pltpu.VMEM((2, s_len, D_LRU), F32),
                pltpu.VMEM((2, s_len, D_LRU), F32),
                pltpu.VMEM((2, s_len, D_LRU), F32),
            ],
        ),
        input_output_aliases=aliases,
        compiler_params=_cparams(1, 48),
        name=f"rglru_{s_len}",
    )(*args)


TM_OUT = 512


def _outproj_kernel(l_ref, att_ref, cv_ref, rec_ref, w_ref, x_ref, gate_ref, g_ref, sh_ref, sc_ref,
                    o_ref, h_ref):
    del l_ref
    y = jnp.dot(att_ref[...], w_ref[0:D_ATT, :], preferred_element_type=F32)
    y = y + jnp.dot(cv_ref[...], w_ref[D_ATT:D_ATT + D_CONV, :], preferred_element_type=F32)
    y = y + jnp.dot(rec_ref[...], w_ref[D_ATT + D_CONV:D_MIX, :], preferred_element_type=F32)
    o_ref[...] = x_ref[...] + gate_ref[...] * y
    gs = g_ref[...] * (1.0 + sc_ref[...])
    shift = sh_ref[...]
    for c in range(TM_OUT // NORM_ROWS):
        rs = slice(c * NORM_ROWS, (c + 1) * NORM_ROWS)
        xn = o_ref[rs, :]
        ms = jnp.mean(xn * xn, axis=-1, keepdims=True)
        h_ref[rs, :] = (xn * lax.rsqrt(ms + EPS) * gs + shift).astype(BF16)


def _outproj(l, att, cv, rec, w_out, x, ln_g, mod4):
    def mod_spec(chunk):
        return pl.BlockSpec((None, None, 1, D_MODEL),
                            lambda t, l: (l[0], _mod_row(t // (TM // TM_OUT)), 0, chunk))

    return pl.pallas_call(
        _outproj_kernel,
        out_shape=(jax.ShapeDtypeStruct((T_ALL, D_MODEL), F32),
                   jax.ShapeDtypeStruct((T_ALL, D_MODEL), BF16)),
        grid_spec=pltpu.PrefetchScalarGridSpec(
            num_scalar_prefetch=1,
            grid=(T_ALL // TM_OUT,),
            in_specs=[
                pl.BlockSpec((TM_OUT, D_ATT), lambda t, l: (t, 0)),
                pl.BlockSpec((TM_OUT, D_CONV), lambda t, l: (t, 0)),
                pl.BlockSpec((TM_OUT, D_LRU), lambda t, l: (t, 0)),
                pl.BlockSpec((None, D_MIX, D_MODEL), lambda t, l: (l[0], 0, 0)),
                pl.BlockSpec((TM_OUT, D_MODEL), lambda t, l: (t, 0)),
                mod_spec(2),
                pl.BlockSpec((None, 1, D_MODEL), lambda t, l: (l[0], 0, 0)),
                mod_spec(3),
                mod_spec(4),
            ],
            out_specs=(pl.BlockSpec((TM_OUT, D_MODEL), lambda t, l: (t, 0)),
                       pl.BlockSpec((TM_OUT, D_MODEL), lambda t, l: (t, 0))),
        ),
        compiler_params=_cparams(1, 48),
        name="outproj",
    )(l, att, cv, rec, w_out, x, mod4, ln_g, mod4, mod4)


TN_DOWN = 512
FFN_SPLIT = 2
FFN_HALO = 16
FFN_AHEAD = 3


def _ffn_kernel(l_ref, x_ref, h_scr, gate_ref, wug_ref, wuv_ref, cwg_ref, cwv_ref,
                cbg_ref, cbv_ref, wd_ref, o_ref, mprev_scr, mnext_scr):
    del l_ref
    i = pl.program_id(0)
    j = pl.program_id(1)

    @pl.when(j == 0)
    def _():
        o_ref[...] = jnp.zeros_like(o_ref)
        seq_len = jnp.where(i < CTX_MT, SEQ, DEC_SEQ)
        pos = lax.broadcasted_iota(jnp.int32, (TM, LANES), 0) & (seq_len - 1)
        mprev_scr[...] = jnp.where(pos == 0, 0.0, 1.0)
        mnext_scr[...] = jnp.where(pos == seq_len - 1, 0.0, 1.0)

    rows = TM // FFN_SPLIT
    def bounds(s):
        return max(s * rows - FFN_HALO, 0), min((s + 1) * rows + FFN_HALO, TM)

    def up(s):
        lo, hi = bounds(s)
        h = h_scr[lo:hi, :]
        return (_mm(h, wug_ref[...]), _mm(h, wuv_ref[...]))

    def conv(u, s, cw_ref, cb_ref):
        lo, hi = bounds(s)
        off = s * rows - lo
        prev = pltpu.roll(u, 1, 0) * jnp.tile(mprev_scr[lo:hi, :], (1, TF // LANES))
        nxt = pltpu.roll(u, hi - lo - 1, 0) * jnp.tile(mnext_scr[lo:hi, :], (1, TF // LANES))
        c = cw_ref[0:1, :] * prev + cw_ref[1:2, :] * u + cw_ref[2:3, :] * nxt + cb_ref[...]
        return c[off:off + rows, :]

    pending = [up(s) for s in range(min(FFN_AHEAD, FFN_SPLIT))]
    for s in range(FFN_SPLIT):
        if s + FFN_AHEAD < FFN_SPLIT:
            pending.append(up(s + FFN_AHEAD))
        u_cur = pending.pop(0)
        ug = conv(u_cur[0], s, cwg_ref, cbg_ref)
        uv = conv(u_cur[1], s, cwv_ref, cbv_ref)
        act = (_silu(ug) * uv).astype(BF16)
        rs = slice(s * rows, (s + 1) * rows)
        for n in range(D_MODEL // TN_DOWN):
            ns = slice(n * TN_DOWN, (n + 1) * TN_DOWN)
            o_ref[rs, ns] = _mm(act, wd_ref[:, ns]) + o_ref[rs, ns]

    @pl.when(j == N_FT - 1)
    def _():
        o_ref[...] = x_ref[...] + gate_ref[...] * o_ref[...]


def _ffn(l, x, h2, mod4, ffn_up, conv_w, conv_b, ffn_down):
    return pl.pallas_call(
        _ffn_kernel,
        out_shape=jax.ShapeDtypeStruct((T_ALL, D_MODEL), F32),
        grid_spec=pltpu.PrefetchScalarGridSpec(
            num_scalar_prefetch=1,
            grid=(N_MT, N_FT),
            in_specs=[
                pl.BlockSpec((TM, D_MODEL), lambda i, j, l: (i, 0), pipeline_mode=pl.Buffered(1)),
                pl.BlockSpec((TM, D_MODEL), lambda i, j, l: (i, 0), pipeline_mode=pl.Buffered(1)),
                _mod_spec(5),
                pl.BlockSpec((None, D_MODEL, TF), lambda i, j, l: (l[0], 0, j)),
                pl.BlockSpec((None, D_MODEL, TF), lambda i, j, l: (l[0], 0, N_FT + j)),
                pl.BlockSpec((None, 3, TF), lambda i, j, l: (l[0], 0, j)),
                pl.BlockSpec((None, 3, TF), lambda i, j, l: (l[0], 0, N_FT + j)),
                pl.BlockSpec((None, 1, TF), lambda i, j, l: (l[0], 0, j)),
                pl.BlockSpec((None, 1, TF), lambda i, j, l: (l[0], 0, N_FT + j)),
                pl.BlockSpec((None, TF, D_MODEL), lambda i, j, l: (l[0], j, 0)),
            ],
            out_specs=pl.BlockSpec((TM, D_MODEL), lambda i, j, l: (i, 0)),
            scratch_shapes=[
                pltpu.VMEM((TM, LANES), F32),
                pltpu.VMEM((TM, LANES), F32),
            ],
        ),
        compiler_params=_cparams(2, 62),
        name="convffn",
    )(l, x, h2, mod4, ffn_up, ffn_up, conv_w, conv_w, conv_b, conv_b, ffn_down)


def _final_kernel(x_ref, g_ref, o_ref):
    g = g_ref[...]
    for c in range(TM_FINAL // NORM_ROWS):
        rs = slice(c * NORM_ROWS, (c + 1) * NORM_ROWS)
        x = x_ref[rs, :]
        ms = jnp.mean(x * x, axis=-1, keepdims=True)
        o_ref[rs, :] = x * lax.rsqrt(ms + EPS) * g


def _final_norm(x, g, row0, n_rows):
    tile0 = row0 // TM_FINAL
    return pl.pallas_call(
        _final_kernel,
        out_shape=jax.ShapeDtypeStruct((n_rows, D_MODEL), F32),
        grid=(n_rows // TM_FINAL,),
        in_specs=[
            pl.BlockSpec((TM_FINAL, D_MODEL), lambda i: (tile0 + i, 0)),
            pl.BlockSpec((1, D_MODEL), lambda i: (0, 0)),
        ],
        out_specs=pl.BlockSpec((TM_FINAL, D_MODEL), lambda i: (i, 0)),
        compiler_params=_cparams(1, 32),
        name="final_norm",
    )(x, g)


def _pack_rows(vecs, width):
    rows = jnp.stack([v.reshape(DEPTH, width) for v in vecs], axis=1)
    rows = jnp.pad(rows, ((0, 0), (0, SUBLANES - rows.shape[1]), (0, 0)))
    return lax.optimization_barrier(rows)


def _block_diag_halves(w):
    nb = LRU_BLOCKS // 2
    w6 = w.reshape(DEPTH, 2, 2, nb, LRU_BW, LRU_BW)
    eye = jnp.eye(nb, dtype=w.dtype)
    dense = jnp.einsum("ldhjab,jk->ldhjakb", w6, eye)
    dense = dense.reshape(DEPTH, 2, 2, LRU_HALF, LRU_HALF).astype(BF16)
    return lax.optimization_barrier(dense)


def kernel(x_prompt, x_sample, cache_k, cache_v, state_lru, c, c_ctx, ln1_g, w_mod, b_mod, w_in, na_bias,
           cv_w, cv_b, cv_ln_g, cv_ln_b, lru_conv_w, lru_conv_b, lru_wa, lru_ba, lru_wi, lru_bi, lru_lam,
           w_out, ln2_g, ffn_up, ffn_conv_w, ffn_conv_b, ffn_down, final_g):
    x = jnp.concatenate([x_prompt.reshape(T_CTX, D_MODEL), x_sample.reshape(T_LAT, D_MODEL)], axis=0)
    cond = jnp.concatenate([c_ctx[None, :], c, jnp.zeros((N_COND - 1 - DEC_BATCH, D_MODEL), F32)], axis=0)
    mod4 = _modulation(cond, w_mod, b_mod).reshape(DEPTH, N_COND, 1, 6 * D_MODEL)
    tt = _bias_table(na_bias)
    ck = cache_k.reshape(DEC_BATCH, DEPTH, PAST_LEN * N_HEADS, HEAD_DIM)
    cvv = cache_v.reshape(DEC_BATCH, DEPTH, PAST_LEN * N_HEADS, HEAD_DIM)
    state = jnp.pad(jnp.swapaxes(state_lru, 0, 1), ((0, 0), (0, 0), (0, SUBLANES - 2), (0, 0)))
    h0_ctx = jnp.zeros((BATCH, SUBLANES, D_LRU), F32)
    lru_p = {
        "lru_conv_w": lru_conv_w,
        "wa_dense": _block_diag_halves(lru_wa),
        "wi_dense": _block_diag_halves(lru_wi),
        "vec": _pack_rows([lru_conv_b, lru_ba[:, 0], lru_ba[:, 1], lru_bi[:, 0], lru_bi[:, 1],
                           lru_lam[:, 0], lru_lam[:, 1]], D_LRU),
    }
    ln1 = ln1_g.reshape(DEPTH, 1, D_MODEL)
    ln2 = ln2_g.reshape(DEPTH, 1, D_MODEL)
    cv_vec = _pack_rows([cv_b, cv_ln_g, cv_ln_b], D_CONV)
    fcb = ffn_conv_b.reshape(DEPTH, 1, 2 * D_FF)
    lat_sb = T_CTX // DEC_SEQ
    w_out = w_out.astype(BF16)

    def layer(carry, xs):
        x, ks, vs, hs = carry
        li, h0_lat = xs
        l = jnp.reshape(li, (1,))
        z = _inproj(l, x, ln1, mod4, w_in)
        att, ks, vs = _att_ctx(l, z, ks, vs)
        att = _att_lat(l, z, ck, cvv, tt, att)
        cvo = _convmod(l, z, cv_w, cv_vec, SEQ, BATCH, 0)
        cvo = _convmod(l, z, cv_w, cv_vec, DEC_SEQ, DEC_BATCH, lat_sb, prev=cvo)
        rec, h_last = _lru(l, z, h0_ctx, lru_p, SEQ, BATCH, 0)
        rec, _ = _lru(l, z, h0_lat, lru_p, DEC_SEQ, DEC_BATCH, lat_sb, prev=rec)
        x, h2 = _outproj(l, att, cvo, rec, w_out, x, ln2, mod4)
        x = _ffn(l, x, h2, mod4, ffn_up, ffn_conv_w, fcb, ffn_down)
        hs = lax.dynamic_update_slice(hs, h_last[:, None, 0:2], (0, li, 0, 0))
        return (x, ks, vs, hs), None

    init = (x,
            jnp.zeros((BATCH, DEPTH, SEQ * N_HEADS, HEAD_DIM), F32),
            jnp.zeros((BATCH, DEPTH, SEQ * N_HEADS, HEAD_DIM), F32),
            jnp.zeros((BATCH, DEPTH, 2, D_LRU), F32))
    (x, ks, vs, hs), _ = lax.scan(layer, init, (jnp.arange(DEPTH, dtype=jnp.int32), state))

    g, x = lax.optimization_barrier((final_g.reshape(1, D_MODEL), x))
    y_prompt = _final_norm(x, g, 0, T_CTX).reshape(BATCH, SEQ, D_MODEL)
    y_sample = _final_norm(x, g, T_CTX, T_LAT).reshape(DEC_BATCH, DEC_SEQ, D_MODEL)
    new_k = ks.reshape(BATCH, DEPTH, SEQ, N_HEADS, HEAD_DIM)
    new_v = vs.reshape(BATCH, DEPTH, SEQ, N_HEADS, HEAD_DIM)
    return (y_prompt, y_sample, new_k, new_v, hs)
```

```python
import functools

import jax
import jax.numpy as jnp
from jax import lax
from jax.experimental import pallas as pl
from jax.experimental.pallas import tpu as pltpu

F32 = jnp.float32
BF16 = jnp.bfloat16

D_MODEL = 2048
BATCH = 16
SEQ = 256
DEPTH = 4
DEC_BATCH = 2
DEC_SEQ = 1024
PAST_LEN = 512
GRID_W = 64
GRID_R = DEC_SEQ // GRID_W
D_ATT = 1024
N_HEADS = 8
HEAD_DIM = 128
D_CONV = 512
D_LRU = 512
D_MIX = D_ATT + D_CONV + D_LRU
D_IN = 3 * D_ATT + 2 * D_CONV + 2 * D_LRU
WIN_ROWS = 8
WIN_COLS = 16
CONV_WIDTH = 31
LRU_CONV_WIDTH = 4
LRU_BLOCKS = 8
LRU_BW = D_LRU // LRU_BLOCKS
LRU_C = 8.0
D_FF = 5632
EPS = 1e-6
ATT_SCALE = HEAD_DIM ** -0.5
NEG_INF = -1e30

T_CTX = BATCH * SEQ
T_LAT = DEC_BATCH * DEC_SEQ
T_ALL = T_CTX + T_LAT

V7X_VMEM_BYTES = 64 * 1024 * 1024
SUBLANES = 8
LANES = 128

TM = 1024
N_MT = T_ALL // TM
CTX_MT = T_CTX // TM
TM2 = 2 * TM
N_MT2 = T_ALL // TM2
TM_FINAL = 256
TN_IN = 512
TF = 512
N_FT = D_FF // TF
TN_MOD = 1024
NORM_ROWS = 128
N_COND = 8
BIAS_ROWS = 2 * WIN_ROWS - 1
BIAS_COLS = 2 * WIN_COLS - 1


def _cparams(n_axes, vmem_mib, flags=None):
    return pltpu.CompilerParams(
        dimension_semantics=("arbitrary",) * n_axes,
        vmem_limit_bytes=min(vmem_mib * 1024 * 1024, V7X_VMEM_BYTES - 2 * 1024 * 1024),
        flags=flags,
    )


def _sigmoid(x):
    return 0.5 * jnp.tanh(0.5 * x) + 0.5


def _silu(x):
    return x * _sigmoid(x)


def _mm(a, b):
    return lax.dot_general(a, b, (((1,), (0,)), ((), ())), preferred_element_type=F32)


def _gelu_tanh(x):
    return 0.5 * x * (1.0 + jnp.tanh(0.7978845608028654 * (x + 0.044715 * (x * x * x))))


def _mod_row(i):
    return jnp.maximum(i - (CTX_MT - 1), 0)


def _mod_kernel(c_ref, w_ref, b_ref, o_ref):
    a = _silu(c_ref[...]).astype(BF16)
    o_ref[...] = jnp.dot(a, w_ref[...].astype(BF16), preferred_element_type=F32) + b_ref[...]


def _modulation(cond, w_mod, b_mod):
    n_out = w_mod.shape[-1]
    return pl.pallas_call(
        _mod_kernel,
        out_shape=jax.ShapeDtypeStruct((DEPTH, N_COND, n_out), F32),
        grid=(DEPTH, n_out // TN_MOD),
        in_specs=[
            pl.BlockSpec((N_COND, D_MODEL), lambda l, j: (0, 0)),
            pl.BlockSpec((None, D_MODEL, TN_MOD), lambda l, j: (l, 0, j)),
            pl.BlockSpec((None, 1, TN_MOD), lambda l, j: (l, 0, j)),
        ],
        out_specs=pl.BlockSpec((None, N_COND, TN_MOD), lambda l, j: (l, 0, j)),
        compiler_params=_cparams(2, 40),
        name="modulation",
    )(cond, w_mod, b_mod.reshape(DEPTH, 1, n_out))


def _bias_kernel(rb_ref, o_ref, td_scr):
    base = (pl.program_id(0) * N_HEADS + pl.program_id(1)) * (BIAS_ROWS * BIAS_COLS)
    qi = lax.broadcasted_iota(jnp.int32, (GRID_W, LANES), 0)
    li = lax.broadcasted_iota(jnp.int32, (GRID_W, LANES), 1)
    diff = jnp.clip((li & (GRID_W - 1)) - qi, -(WIN_COLS - 1), WIN_COLS - 1) + (WIN_COLS - 1)

    def body(i, carry):
        acc = jnp.zeros((GRID_W, LANES), F32)
        for d in range(BIAS_COLS):
            acc = jnp.where(diff == d, rb_ref[base + i * BIAS_COLS + d], acc)
        td_scr[i] = acc
        return carry

    lax.fori_loop(0, BIAS_ROWS, body, 0)
    td_scr[BIAS_ROWS] = jnp.zeros((GRID_W, LANES), F32)
    left = li < GRID_W
    for i in range(BIAS_ROWS):
        o_ref[i] = jnp.where(left, td_scr[i], td_scr[i + 1])


def _bias_table(na_bias):
    return pl.pallas_call(
        _bias_kernel,
        out_shape=jax.ShapeDtypeStruct((DEPTH, N_HEADS, BIAS_ROWS, GRID_W, LANES), F32),
        grid=(DEPTH, N_HEADS),
        in_specs=[pl.BlockSpec(memory_space=pltpu.SMEM)],
        out_specs=pl.BlockSpec((None, None, BIAS_ROWS, GRID_W, LANES), lambda l, h: (l, h, 0, 0, 0)),
        scratch_shapes=[pltpu.VMEM((BIAS_ROWS + 1, GRID_W, LANES), F32)],
        compiler_params=_cparams(2, 16),
        name="bias_table",
    )(na_bias.reshape(-1))


def _norm_mod(x_ref, g_ref, sc_ref, sh_ref, h_scr, row0=0):
    gs = g_ref[...] * (1.0 + sc_ref[...])
    shift = sh_ref[...]

    def body(c, carry):
        r0 = pl.multiple_of(row0 + c * NORM_ROWS, NORM_ROWS)
        x = x_ref[pl.ds(r0, NORM_ROWS), :]
        ms = jnp.mean(x * x, axis=-1, keepdims=True)
        h_scr[pl.ds(r0, NORM_ROWS), :] = (x * lax.rsqrt(ms + EPS) * gs + shift).astype(BF16)
        return carry

    lax.fori_loop(0, TM // NORM_ROWS, body, 0)


def _mod_spec(chunk):
    return pl.BlockSpec((None, None, 1, D_MODEL), lambda i, j, l: (l[0], _mod_row(i), 0, chunk))


def _mod_spec2(chunk, half):
    return pl.BlockSpec((None, None, 1, D_MODEL), lambda i, j, l: (l[0], _mod_row(2 * i + half), 0, chunk))


def _inproj_kernel(l_ref, x_ref, g_ref, sh0_ref, sc0_ref, sh1_ref, sc1_ref, w_ref, o_ref, h_scr):
    @pl.when(pl.program_id(1) == 0)
    def _():
        _norm_mod(x_ref, g_ref, sc0_ref, sh0_ref, h_scr, 0)
        _norm_mod(x_ref, g_ref, sc1_ref, sh1_ref, h_scr, TM)

    o_ref[...] = _mm(h_scr[...], w_ref[...])


def _inproj(l, x, ln_g, mod4, w_in):
    return pl.pallas_call(
        _inproj_kernel,
        out_shape=jax.ShapeDtypeStruct((T_ALL, D_IN), F32),
        grid_spec=pltpu.PrefetchScalarGridSpec(
            num_scalar_prefetch=1,
            grid=(N_MT2, D_IN // TN_IN),
            in_specs=[
                pl.BlockSpec((TM2, D_MODEL), lambda i, j, l: (i, 0)),
                pl.BlockSpec((None, 1, D_MODEL), lambda i, j, l: (l[0], 0, 0)),
                _mod_spec2(0, 0),
                _mod_spec2(1, 0),
                _mod_spec2(0, 1),
                _mod_spec2(1, 1),
                pl.BlockSpec((None, D_MODEL, TN_IN), lambda i, j, l: (l[0], 0, j)),
            ],
            out_specs=pl.BlockSpec((TM2, TN_IN), lambda i, j, l: (i, j)),
            scratch_shapes=[pltpu.VMEM((TM2, D_MODEL), BF16)],
        ),
        compiler_params=_cparams(2, 62),
        name="inproj",
    )(l, x, ln_g, mod4, mod4, mod4, mod4, w_in)


def _dot_nt(a, b):
    return lax.dot_general(a, b, (((1,), (1,)), ((), ())), preferred_element_type=F32)


def _att_ctx_kernel(l_ref, q_ref, k_ref, v_ref, *rest):
    del l_ref
    o_ref, nk_ref, nv_ref = rest[-3:]
    for h in range(N_HEADS):
        sl = slice(h * HEAD_DIM, (h + 1) * HEAD_DIM)
        kf = k_ref[:, sl]
        vf = v_ref[:, sl]
        nk_ref[pl.ds(h, SEQ, stride=N_HEADS), :] = kf
        nv_ref[pl.ds(h, SEQ, stride=N_HEADS), :] = vf
        q = q_ref[:, sl].astype(BF16)
        k = kf.astype(BF16)
        v = vf.astype(BF16)
        s = _dot_nt(q, k) * ATT_SCALE
        p = jnp.exp(s - jnp.max(s, axis=-1, keepdims=True))
        denom = jnp.sum(p, axis=-1, keepdims=True)
        o = jnp.dot(p.astype(BF16), v, preferred_element_type=F32)
        o_ref[:, sl] = (o / denom).astype(o_ref.dtype)


def _att_ctx(l, z, ks=None, vs=None):
    kv_shape = jax.ShapeDtypeStruct((BATCH, DEPTH, SEQ * N_HEADS, HEAD_DIM), F32)
    kv_spec = pl.BlockSpec((None, None, SEQ * N_HEADS, HEAD_DIM), lambda b, l: (b, l[0], 0, 0))
    in_specs = [
        pl.BlockSpec((SEQ, D_ATT), lambda b, l: (b, 0)),
        pl.BlockSpec((SEQ, D_ATT), lambda b, l: (b, 1)),
        pl.BlockSpec((SEQ, D_ATT), lambda b, l: (b, 2)),
    ]
    args = [l, z, z, z]
    aliases = {}
    if ks is not None:
        in_specs += [pl.BlockSpec(memory_space=pl.ANY), pl.BlockSpec(memory_space=pl.ANY)]
        args += [ks, vs]
        aliases = {4: 1, 5: 2}
    return pl.pallas_call(
        _att_ctx_kernel,
        out_shape=(jax.ShapeDtypeStruct((T_ALL, D_ATT), BF16), kv_shape, kv_shape),
        grid_spec=pltpu.PrefetchScalarGridSpec(
            num_scalar_prefetch=1,
            grid=(BATCH,),
            in_specs=in_specs,
            out_specs=(pl.BlockSpec((SEQ, D_ATT), lambda b, l: (b, 0)), kv_spec, kv_spec),
        ),
        input_output_aliases=aliases,
        compiler_params=_cparams(1, 32),
        name="att_ctx",
    )(*args)


Q_ROWS = 4
_KEY_ROWS = ((0, 8), (0, 12), (4, 16), (8, 16))


def _win_start(rq):
    return min(max(rq - WIN_ROWS // 2, 0), GRID_R - WIN_ROWS)


def _att_lat_kernel(l_ref, q_ref, k_ref, v_ref, ck_ref, cv_ref, tt_ref, att_in_ref, o_ref):
    del l_ref, att_in_ref
    qi = lax.broadcasted_iota(jnp.int32, (GRID_W, LANES), 0)
    li = lax.broadcasted_iota(jnp.int32, (GRID_W, LANES), 1)
    kc = li & (GRID_W - 1)
    cs = jnp.clip(qi - WIN_COLS // 2, 0, GRID_W - WIN_COLS)
    in_cols = (kc >= cs) & (kc < cs + WIN_COLS)
    left = li < GRID_W
    mask_both = in_cols
    mask_left = in_cols & left
    mask_right = in_cols & jnp.logical_not(left)
    neg = jnp.full((GRID_W, LANES), NEG_INF, F32)

    kb = k_ref[...].astype(BF16)
    vb = v_ref[...].astype(BF16)
    head = pl.program_id(1)
    ckb = ck_ref[pl.ds(head, PAST_LEN, stride=N_HEADS), :].astype(BF16)
    cvb = cv_ref[pl.ds(head, PAST_LEN, stride=N_HEADS), :].astype(BF16)
    tq = Q_ROWS * GRID_W
    n_tiles = GRID_R // Q_ROWS

    def scores(t):
        lo, hi = _KEY_ROWS[t]
        q = q_ref[t * tq:(t + 1) * tq, :].astype(BF16)
        return _dot_nt(q, kb[lo * GRID_W:hi * GRID_W, :]), _dot_nt(q, ckb)

    nxt = scores(0)
    for t in range(n_tiles):
        lo, hi = _KEY_ROWS[t]
        s_w, s_c = nxt
        if t + 1 < n_tiles:
            nxt = scores(t + 1)
        s_c = s_c * ATT_SCALE
        rows = []
        for rl in range(Q_ROWS):
            rq = t * Q_ROWS + rl
            start = _win_start(rq)
            blks = []
            for m in range(lo // 2, hi // 2):
                v0 = start <= 2 * m < start + WIN_ROWS
                v1 = start <= 2 * m + 1 < start + WIN_ROWS
                if not (v0 or v1):
                    blks.append(neg)
                    continue
                i = 2 * m - rq + WIN_ROWS - 1
                assert 0 <= i < BIAS_ROWS
                mask = mask_both if (v0 and v1) else (mask_left if v0 else mask_right)
                c0 = (2 * m - lo) * GRID_W
                sb = s_w[rl * GRID_W:(rl + 1) * GRID_W, c0:c0 + LANES]
                blks.append(jnp.where(mask, sb * ATT_SCALE + tt_ref[i], NEG_INF))
            rows.append(jnp.concatenate(blks, axis=1))
        sw = jnp.concatenate(rows, axis=0)
        mx = jnp.maximum(jnp.max(sw, axis=-1, keepdims=True), jnp.max(s_c, axis=-1, keepdims=True))
        pw = jnp.exp(sw - mx)
        pc = jnp.exp(s_c - mx)
        denom = jnp.sum(pw, axis=-1, keepdims=True) + jnp.sum(pc, axis=-1, keepdims=True)
        o = jnp.dot(pw.astype(BF16), vb[lo * GRID_W:hi * GRID_W, :], preferred_element_type=F32)
        o = o + jnp.dot(pc.astype(BF16), cvb, preferred_element_type=F32)
        o_ref[t * tq:(t + 1) * tq, :] = (o / denom).astype(o_ref.dtype)


def _att_lat(l, z, cache_k, cache_v, tt, att):
    rb0 = T_CTX // DEC_SEQ
    hb = D_ATT // HEAD_DIM
    return pl.pallas_call(
        _att_lat_kernel,
        out_shape=jax.ShapeDtypeStruct((T_ALL, D_ATT), BF16),
        grid_spec=pltpu.PrefetchScalarGridSpec(
            num_scalar_prefetch=1,
            grid=(DEC_BATCH, N_HEADS),
            in_specs=[
                pl.BlockSpec((DEC_SEQ, HEAD_DIM), lambda b, h, l: (rb0 + b, h)),
                pl.BlockSpec((DEC_SEQ, HEAD_DIM), lambda b, h, l: (rb0 + b, hb + h)),
                pl.BlockSpec((DEC_SEQ, HEAD_DIM), lambda b, h, l: (rb0 + b, 2 * hb + h)),
                pl.BlockSpec((None, None, PAST_LEN * N_HEADS, HEAD_DIM), lambda b, h, l: (b, l[0], 0, 0)),
                pl.BlockSpec((None, None, PAST_LEN * N_HEADS, HEAD_DIM), lambda b, h, l: (b, l[0], 0, 0)),
                pl.BlockSpec((None, None, BIAS_ROWS, GRID_W, LANES), lambda b, h, l: (l[0], h, 0, 0, 0)),
                pl.BlockSpec(memory_space=pl.ANY),
            ],
            out_specs=pl.BlockSpec((DEC_SEQ, HEAD_DIM), lambda b, h, l: (rb0 + b, h)),
        ),
        input_output_aliases={7: 0},
        compiler_params=_cparams(2, 32),
        name="att_lat",
    )(l, z, z, z, cache_k, cache_v, tt, att)


CV_PAD = 16
CV_ROWS = 64


def _convmod_kernel(l_ref, a_ref, g_ref, w_ref, vec_ref, *rest):
    del l_ref
    o_ref, upad, ush = rest[-3:]
    s_len = a_ref.shape[0]
    n_sh = s_len + CV_PAD + SUBLANES
    upad[0:CV_PAD, :] = jnp.zeros((CV_PAD, D_CONV), F32)
    upad[CV_PAD + s_len:, :] = jnp.zeros((CV_PAD, D_CONV), F32)
    upad[CV_PAD:CV_PAD + s_len, :] = a_ref[...] * _sigmoid(g_ref[...])
    for j in range(SUBLANES):
        ush[j, 0:n_sh, :] = upad[j:j + n_sh, :]
    bias = vec_ref[0:1, :]
    ln_g = vec_ref[1:2, :]
    ln_b = vec_ref[2:3, :]

    def chunk(c, carry):
        r0 = pl.multiple_of(c * CV_ROWS, CV_ROWS)
        acc = jnp.broadcast_to(bias, (CV_ROWS, D_CONV))
        for k in range(CONV_WIDTH):
            m, j = divmod(k + CV_PAD - CONV_WIDTH // 2, SUBLANES)
            acc = acc + w_ref[k:k + 1, :] * ush[j, pl.ds(r0 + SUBLANES * m, CV_ROWS), :]
        mu = jnp.mean(acc, axis=-1, keepdims=True)
        xc = acc - mu
        var = jnp.mean(xc * xc, axis=-1, keepdims=True)
        y = xc * lax.rsqrt(var + EPS) * ln_g + ln_b
        o_ref[pl.ds(r0, CV_ROWS), :] = _silu(y).astype(o_ref.dtype)
        return carry

    lax.fori_loop(0, s_len // CV_ROWS, chunk, 0)


def _convmod(l, z, cv_w, cv_vec, s_len, n_seq, row_block0, prev=None):
    cb = (3 * D_ATT) // D_CONV
    in_specs = [
        pl.BlockSpec((s_len, D_CONV), lambda b, l: (row_block0 + b, cb)),
        pl.BlockSpec((s_len, D_CONV), lambda b, l: (row_block0 + b, cb + 1)),
        pl.BlockSpec((None, CONV_WIDTH, D_CONV), lambda b, l: (l[0], 0, 0)),
        pl.BlockSpec((None, SUBLANES, D_CONV), lambda b, l: (l[0], 0, 0)),
    ]
    args = [l, z, z, cv_w, cv_vec]
    aliases = {}
    if prev is not None:
        in_specs.append(pl.BlockSpec(memory_space=pl.ANY))
        args.append(prev)
        aliases = {len(args) - 1: 0}
    return pl.pallas_call(
        _convmod_kernel,
        out_shape=jax.ShapeDtypeStruct((T_ALL, D_CONV), BF16),
        grid_spec=pltpu.PrefetchScalarGridSpec(
            num_scalar_prefetch=1,
            grid=(n_seq,),
            in_specs=in_specs,
            out_specs=pl.BlockSpec((s_len, D_CONV), lambda b, l: (row_block0 + b, 0)),
            scratch_shapes=[
                pltpu.VMEM((s_len + 2 * CV_PAD, D_CONV), F32),
                pltpu.VMEM((SUBLANES, s_len + CV_PAD + SUBLANES, D_CONV), F32),
            ],
        ),
        input_output_aliases=aliases,
        compiler_params=_cparams(1, 48),
        name=f"convmod_{s_len}",
    )(*args)


LRU_PAD = 8
LRU_HALF = D_LRU // 2
LRU_CHUNK = 256


def _softplus(x):
    return jnp.maximum(x, 0.0) + jnp.log1p(jnp.exp(-jnp.abs(x)))


def _lru_kernel(l_ref, x_ref, g_ref, h0_ref, cw_ref, wa_ref, wi_ref, vec_ref, *rest):
    del l_ref
    cb_ref = vec_ref.at[0:1]
    ba_ref = vec_ref.at[1:3]
    bi_ref = vec_ref.at[3:5]
    lam_ref = vec_ref.at[5:7]
    y_ref, hl_ref, xpad, a_scr, b_scr, h_scr = rest[-6:]
    s_len = x_ref.shape[0]
    n_blk = s_len // SUBLANES
    n_lt = D_LRU // LANES

    xpad[0:LRU_PAD, :] = jnp.zeros((LRU_PAD, D_LRU), F32)
    xpad[LRU_PAD + s_len:, :] = jnp.zeros((LRU_PAD, D_LRU), F32)
    xpad[LRU_PAD:LRU_PAD + s_len, :] = x_ref[...]
    cb = cb_ref[...]
    decay = [-LRU_C * _softplus(-lam_ref[d:d + 1, :]) for d in range(2)]

    for c in range(s_len // LRU_CHUNK):
        r0 = c * LRU_CHUNK
        xc = jnp.broadcast_to(cb, (LRU_CHUNK, D_LRU))
        for k in range(LRU_CONV_WIDTH):
            off = LRU_PAD + r0 + k - LRU_CONV_WIDTH // 2
            xc = xc + cw_ref[k:k + 1, :] * xpad[off:off + LRU_CHUNK, :]
        xb = xc.astype(BF16)
        for d in range(2):
            ga = jnp.concatenate(
                [jnp.dot(xb[:, hh * LRU_HALF:(hh + 1) * LRU_HALF], wa_ref[d, hh], preferred_element_type=F32)
                 for hh in range(2)], axis=1) + ba_ref[d:d + 1, :]
            gi = jnp.concatenate(
                [jnp.dot(xb[:, hh * LRU_HALF:(hh + 1) * LRU_HALF], wi_ref[d, hh], preferred_element_type=F32)
                 for hh in range(2)], axis=1) + bi_ref[d:d + 1, :]
            log_a = _sigmoid(ga) * decay[d]
            th = jnp.tanh(log_a)
            gain = jnp.sqrt(-2.0 * th / (1.0 - th))
            a_scr[d, r0:r0 + LRU_CHUNK, :] = jnp.exp(log_a)
            b_scr[d, r0:r0 + LRU_CHUNK, :] = gain * (_sigmoid(gi) * xc)

    row = lax.broadcasted_iota(jnp.int32, (SUBLANES, LANES), 0)

    def scan_block(it, carry):
        new = []
        for d in range(2):
            blk = it if d == 0 else n_blk - 1 - it
            r0 = pl.multiple_of(blk * SUBLANES, SUBLANES)
            for c in range(n_lt):
                ls = slice(c * LANES, (c + 1) * LANES)
                a = a_scr[d, pl.ds(r0, SUBLANES), ls]
                b = b_scr[d, pl.ds(r0, SUBLANES), ls]
                for s in (1, 2, 4):
                    shift = s if d == 0 else SUBLANES - s
                    valid = (row >= s) if d == 0 else (row < SUBLANES - s)
                    a_sh = jnp.where(valid, pltpu.roll(a, shift, 0), 1.0)
                    b_sh = jnp.where(valid, pltpu.roll(b, shift, 0), 0.0)
                    b = a * b_sh + b
                    a = a * a_sh
                h = a * carry[d * n_lt + c] + b
                h_scr[d, pl.ds(r0, SUBLANES), ls] = h
                last = h[SUBLANES - 1:SUBLANES, :] if d == 0 else h[0:1, :]
                new.append(jnp.broadcast_to(last, (SUBLANES, LANES)))
        return tuple(new)

    init = tuple(
        jnp.broadcast_to(h0_ref[d:d + 1, c * LANES:(c + 1) * LANES], (SUBLANES, LANES))
        for d in range(2) for c in range(n_lt))
    final = lax.fori_loop(0, n_blk, scan_block, init)
    hl_ref[...] = jnp.zeros(hl_ref.shape, F32)
    for d in range(2):
        for c in range(n_lt):
            hl_ref[d:d + 1, c * LANES:(c + 1) * LANES] = final[d * n_lt + c][0:1, :]

    for c in range(s_len // LRU_CHUNK):
        rs = slice(c * LRU_CHUNK, (c + 1) * LRU_CHUNK)
        y = (h_scr[0, rs, :] + h_scr[1, rs, :]) * _gelu_tanh(g_ref[rs, :])
        y_ref[rs, :] = y.astype(y_ref.dtype)


def _lru(l, z, h0, p, s_len, n_seq, row_block0, prev=None):
    cb = (3 * D_ATT + 2 * D_CONV) // D_LRU
    in_specs = [
        pl.BlockSpec((s_len, D_LRU), lambda b, l: (row_block0 + b, cb)),
        pl.BlockSpec((s_len, D_LRU), lambda b, l: (row_block0 + b, cb + 1)),
        pl.BlockSpec((None, SUBLANES, D_LRU), lambda b, l: (b, 0, 0)),
        pl.BlockSpec((None, LRU_CONV_WIDTH, D_LRU), lambda b, l: (l[0], 0, 0)),
        pl.BlockSpec((None, 2, 2, LRU_HALF, LRU_HALF), lambda b, l: (l[0], 0, 0, 0, 0)),
        pl.BlockSpec((None, 2, 2, LRU_HALF, LRU_HALF), lambda b, l: (l[0], 0, 0, 0, 0)),
        pl.BlockSpec((None, SUBLANES, D_LRU), lambda b, l: (l[0], 0, 0)),
    ]
    args = [l, z, z, h0, p["lru_conv_w"], p["wa_dense"], p["wi_dense"], p["vec"]]
    aliases = {}
    if prev is not None:
        in_specs.append(pl.BlockSpec(memory_space=pl.ANY))
        args.append(prev)
        aliases = {len(args) - 1: 0}
    return pl.pallas_call(
        _lru_kernel,
        out_shape=(jax.ShapeDtypeStruct((T_ALL, D_LRU), BF16),
                   jax.ShapeDtypeStruct((n_seq, SUBLANES, D_LRU), F32)),
        grid_spec=pltpu.PrefetchScalarGridSpec(
            num_scalar_prefetch=1,
            grid=(n_seq,),
            in_specs=in_specs,
            out_specs=(pl.BlockSpec((s_len, D_LRU), lambda b, l: (row_block0 + b, 0)),
                       pl.BlockSpec((None, SUBLANES, D_LRU), lambda b, l: (b, 0, 0))),
            scratch_shapes=[
                pltpu.VMEM((s_len + 2 * LRU_PAD, D_LRU), F32),
                pltpu.VMEM((2, s_len, D_LRU), F32),
                pltpu.VMEM((2, s_len, D_LRU), F32),
                pltpu.VMEM((2, s_len, D_LRU), F32),
            ],
        ),
        input_output_aliases=aliases,
        compiler_params=_cparams(1, 48),
        name=f"rglru_{s_len}",
    )(*args)


TM_OUT = 512


def _outproj_kernel(l_ref, att_ref, cv_ref, rec_ref, w_ref, x_ref, gate_ref, g_ref, sh_ref, sc_ref,
                    o_ref, h_ref):
    del l_ref
    y = jnp.dot(att_ref[...], w_ref[0:D_ATT, :], preferred_element_type=F32)
    y = y + jnp.dot(cv_ref[...], w_ref[D_ATT:D_ATT + D_CONV, :], preferred_element_type=F32)
    y = y + jnp.dot(rec_ref[...], w_ref[D_ATT + D_CONV:D_MIX, :], preferred_element_type=F32)
    o_ref[...] = x_ref[...] + gate_ref[...] * y
    gs = g_ref[...] * (1.0 + sc_ref[...])
    shift = sh_ref[...]
    for c in range(TM_OUT // NORM_ROWS):
        rs = slice(c * NORM_ROWS, (c + 1) * NORM_ROWS)
        xn = o_ref[rs, :]
        ms = jnp.mean(xn * xn, axis=-1, keepdims=True)
        h_ref[rs, :] = (xn * lax.rsqrt(ms + EPS) * gs + shift).astype(BF16)


def _outproj(l, att, cv, rec, w_out, x, ln_g, mod4):
    def mod_spec(chunk):
        return pl.BlockSpec((None, None, 1, D_MODEL),
                            lambda t, l: (l[0], _mod_row(t // (TM // TM_OUT)), 0, chunk))

    return pl.pallas_call(
        _outproj_kernel,
        out_shape=(jax.ShapeDtypeStruct((T_ALL, D_MODEL), F32),
                   jax.ShapeDtypeStruct((T_ALL, D_MODEL), BF16)),
        grid_spec=pltpu.PrefetchScalarGridSpec(
            num_scalar_prefetch=1,
            grid=(T_ALL // TM_OUT,),
            in_specs=[
                pl.BlockSpec((TM_OUT, D_ATT), lambda t, l: (t, 0)),
                pl.BlockSpec((TM_OUT, D_CONV), lambda t, l: (t, 0)),
                pl.BlockSpec((TM_OUT, D_LRU), lambda t, l: (t, 0)),
                pl.BlockSpec((None, D_MIX, D_MODEL), lambda t, l: (l[0], 0, 0)),
                pl.BlockSpec((TM_OUT, D_MODEL), lambda t, l: (t, 0)),
                mod_spec(2),
                pl.BlockSpec((None, 1, D_MODEL), lambda t, l: (l[0], 0, 0)),
                mod_spec(3),
                mod_spec(4),
            ],
            out_specs=(pl.BlockSpec((TM_OUT, D_MODEL), lambda t, l: (t, 0)),
                       pl.BlockSpec((TM_OUT, D_MODEL), lambda t, l: (t, 0))),
        ),
        compiler_params=_cparams(1, 48),
        name="outproj",
    )(l, att, cv, rec, w_out, x, mod4, ln_g, mod4, mod4)


TN_DOWN = 512
FFN_SPLIT = 2
FFN_HALO = 16
FFN_AHEAD = 3


def _ffn_kernel(l_ref, x_ref, h_scr, gate_ref, wug_ref, wuv_ref, cwg_ref, cwv_ref,
                cbg_ref, cbv_ref, wd_ref, o_ref, mprev_scr, mnext_scr):
    del l_ref
    i = pl.program_id(0)
    j = pl.program_id(1)

    @pl.when(j == 0)
    def _():
        o_ref[...] = jnp.zeros_like(o_ref)
        seq_len = jnp.where(i < CTX_MT, SEQ, DEC_SEQ)
        pos = lax.broadcasted_iota(jnp.int32, (TM, LANES), 0) & (seq_len - 1)
        mprev_scr[...] = jnp.where(pos == 0, 0.0, 1.0)
        mnext_scr[...] = jnp.where(pos == seq_len - 1, 0.0, 1.0)

    rows = TM // FFN_SPLIT
    def bounds(s):
        return max(s * rows - FFN_HALO, 0), min((s + 1) * rows + FFN_HALO, TM)

    def up(s):
        lo, hi = bounds(s)
        h = h_scr[lo:hi, :]
        return (_mm(h, wug_ref[...]), _mm(h, wuv_ref[...]))

    def conv(u, s, cw_ref, cb_ref):
        lo, hi = bounds(s)
        off = s * rows - lo
        prev = pltpu.roll(u, 1, 0) * jnp.tile(mprev_scr[lo:hi, :], (1, TF // LANES))
        nxt = pltpu.roll(u, hi - lo - 1, 0) * jnp.tile(mnext_scr[lo:hi, :], (1, TF // LANES))
        c = cw_ref[0:1, :] * prev + cw_ref[1:2, :] * u + cw_ref[2:3, :] * nxt + cb_ref[...]
        return c[off:off + rows, :]

    pending = [up(s) for s in range(min(FFN_AHEAD, FFN_SPLIT))]
    for s in range(FFN_SPLIT):
        if s + FFN_AHEAD < FFN_SPLIT:
            pending.append(up(s + FFN_AHEAD))
        u_cur = pending.pop(0)
        ug = conv(u_cur[0], s, cwg_ref, cbg_ref)
        uv = conv(u_cur[1], s, cwv_ref, cbv_ref)
        act = (_silu(ug) * uv).astype(BF16)
        rs = slice(s * rows, (s + 1) * rows)
        for n in range(D_MODEL // TN_DOWN):
            ns = slice(n * TN_DOWN, (n + 1) * TN_DOWN)
            o_ref[rs, ns] = _mm(act, wd_ref[:, ns]) + o_ref[rs, ns]

    @pl.when(j == N_FT - 1)
    def _():
        o_ref[...] = x_ref[...] + gate_ref[...] * o_ref[...]


def _ffn(l, x, h2, mod4, ffn_up, conv_w, conv_b, ffn_down):
    return pl.pallas_call(
        _ffn_kernel,
        out_shape=jax.ShapeDtypeStruct((T_ALL, D_MODEL), F32),
        grid_spec=pltpu.PrefetchScalarGridSpec(
            num_scalar_prefetch=1,
            grid=(N_MT, N_FT),
            in_specs=[
                pl.BlockSpec((TM, D_MODEL), lambda i, j, l: (i, 0), pipeline_mode=pl.Buffered(1)),
                pl.BlockSpec((TM, D_MODEL), lambda i, j, l: (i, 0), pipeline_mode=pl.Buffered(1)),
                _mod_spec(5),
                pl.BlockSpec((None, D_MODEL, TF), lambda i, j, l: (l[0], 0, j)),
                pl.BlockSpec((None, D_MODEL, TF), lambda i, j, l: (l[0], 0, N_FT + j)),
                pl.BlockSpec((None, 3, TF), lambda i, j, l: (l[0], 0, j)),
                pl.BlockSpec((None, 3, TF), lambda i, j, l: (l[0], 0, N_FT + j)),
                pl.BlockSpec((None, 1, TF), lambda i, j, l: (l[0], 0, j)),
                pl.BlockSpec((None, 1, TF), lambda i, j, l: (l[0], 0, N_FT + j)),
                pl.BlockSpec((None, TF, D_MODEL), lambda i, j, l: (l[0], j, 0)),
            ],
            out_specs=pl.BlockSpec((TM, D_MODEL), lambda i, j, l: (i, 0)),
            scratch_shapes=[
                pltpu.VMEM((TM, LANES), F32),
                pltpu.VMEM((TM, LANES), F32),
            ],
        ),
        compiler_params=_cparams(2, 62),
        name="convffn",
    )(l, x, h2, mod4, ffn_up, ffn_up, conv_w, conv_w, conv_b, conv_b, ffn_down)


def _final_kernel(x_ref, g_ref, o_ref):
    g = g_ref[...]
    for c in range(TM_FINAL // NORM_ROWS):
        rs = slice(c * NORM_ROWS, (c + 1) * NORM_ROWS)
        x = x_ref[rs, :]
        ms = jnp.mean(x * x, axis=-1, keepdims=True)
        o_ref[rs, :] = x * lax.rsqrt(ms + EPS) * g


def _final_norm(x, g, row0, n_rows):
    tile0 = row0 // TM_FINAL
    return pl.pallas_call(
        _final_kernel,
        out_shape=jax.ShapeDtypeStruct((n_rows, D_MODEL), F32),
        grid=(n_rows // TM_FINAL,),
        in_specs=[
            pl.BlockSpec((TM_FINAL, D_MODEL), lambda i: (tile0 + i, 0)),
            pl.BlockSpec((1, D_MODEL), lambda i: (0, 0)),
        ],
        out_specs=pl.BlockSpec((TM_FINAL, D_MODEL), lambda i: (i, 0)),
        compiler_params=_cparams(1, 32),
        name="final_norm",
    )(x, g)


def _pack_rows(vecs, width):
    rows = jnp.stack([v.reshape(DEPTH, width) for v in vecs], axis=1)
    rows = jnp.pad(rows, ((0, 0), (0, SUBLANES - rows.shape[1]), (0, 0)))
    return lax.optimization_barrier(rows)


def _block_diag_halves(w):
    nb = LRU_BLOCKS // 2
    w6 = w.reshape(DEPTH, 2, 2, nb, LRU_BW, LRU_BW)
    eye = jnp.eye(nb, dtype=w.dtype)
    dense = jnp.einsum("ldhjab,jk->ldhjakb", w6, eye)
    dense = dense.reshape(DEPTH, 2, 2, LRU_HALF, LRU_HALF).astype(BF16)
    return lax.optimization_barrier(dense)


def kernel(x_prompt, x_sample, cache_k, cache_v, state_lru, c, c_ctx, ln1_g, w_mod, b_mod, w_in, na_bias,
           cv_w, cv_b, cv_ln_g, cv_ln_b, lru_conv_w, lru_conv_b, lru_wa, lru_ba, lru_wi, lru_bi, lru_lam,
           w_out, ln2_g, ffn_up, ffn_conv_w, ffn_conv_b, ffn_down, final_g):
    x = jnp.concatenate([x_prompt.reshape(T_CTX, D_MODEL), x_sample.reshape(T_LAT, D_MODEL)], axis=0)
    cond = jnp.concatenate([c_ctx[None, :], c, jnp.zeros((N_COND - 1 - DEC_BATCH, D_MODEL), F32)], axis=0)
    mod4 = _modulation(cond, w_mod, b_mod).reshape(DEPTH, N_COND, 1, 6 * D_MODEL)
    tt = _bias_table(na_bias)
    ck = cache_k.reshape(DEC_BATCH, DEPTH, PAST_LEN * N_HEADS, HEAD_DIM)
    cvv = cache_v.reshape(DEC_BATCH, DEPTH, PAST_LEN * N_HEADS, HEAD_DIM)
    state = jnp.pad(jnp.swapaxes(state_lru, 0, 1), ((0, 0), (0, 0), (0, SUBLANES - 2), (0, 0)))
    h0_ctx = jnp.zeros((BATCH, SUBLANES, D_LRU), F32)
    lru_p = {
        "lru_conv_w": lru_conv_w,
        "wa_dense": _block_diag_halves(lru_wa),
        "wi_dense": _block_diag_halves(lru_wi),
        "vec": _pack_rows([lru_conv_b, lru_ba[:, 0], lru_ba[:, 1], lru_bi[:, 0], lru_bi[:, 1],
                           lru_lam[:, 0], lru_lam[:, 1]], D_LRU),
    }
    ln1 = ln1_g.reshape(DEPTH, 1, D_MODEL)
    ln2 = ln2_g.reshape(DEPTH, 1, D_MODEL)
    cv_vec = _pack_rows([cv_b, cv_ln_g, cv_ln_b], D_CONV)
    fcb = ffn_conv_b.reshape(DEPTH, 1, 2 * D_FF)
    lat_sb = T_CTX // DEC_SEQ
    w_out = w_out.astype(BF16)

    def layer(carry, xs):
        x, ks, vs, hs = carry
        li, h0_lat = xs
        l = jnp.reshape(li, (1,))
        z = _inproj(l, x, ln1, mod4, w_in)
        att, ks, vs = _att_ctx(l, z, ks, vs)
        att = _att_lat(l, z, ck, cvv, tt, att)
        cvo = _convmod(l, z, cv_w, cv_vec, SEQ, BATCH, 0)
        cvo = _convmod(l, z, cv_w, cv_vec, DEC_SEQ, DEC_BATCH, lat_sb, prev=cvo)
        rec, h_last = _lru(l, z, h0_ctx, lru_p, SEQ, BATCH, 0)
        rec, _ = _lru(l, z, h0_lat, lru_p, DEC_SEQ, DEC_BATCH, lat_sb, prev=rec)
        x, h2 = _outproj(l, att, cvo, rec, w_out, x, ln2, mod4)
        x = _ffn(l, x, h2, mod4, ffn_up, ffn_conv_w, fcb, ffn_down)
        hs = lax.dynamic_update_slice(hs, h_last[:, None, 0:2], (0, li, 0, 0))
        return (x, ks, vs, hs), None

    first, _ = layer((x, None, None, jnp.zeros((BATCH, DEPTH, 2, D_LRU), F32)), (jnp.int32(0), state[0]))
    (x, ks, vs, hs), _ = lax.scan(layer, first, (jnp.arange(1, DEPTH, dtype=jnp.int32), state[1:]))

    g, x = lax.optimization_barrier((final_g.reshape(1, D_MODEL), x))
    y_prompt = _final_norm(x, g, 0, T_CTX).reshape(BATCH, SEQ, D_MODEL)
    y_sample = _final_norm(x, g, T_CTX, T_LAT).reshape(DEC_BATCH, DEC_SEQ, D_MODEL)
    new_k = ks.reshape(BATCH, DEPTH, SEQ, N_HEADS, HEAD_DIM)
    new_v = vs.reshape(BATCH, DEPTH, SEQ, N_HEADS, HEAD_DIM)
    return (y_prompt, y_sample, new_k, new_v, hs)
```
